```python
import jax, jax.numpy as jnp
from jax import lax
import numpy as np

D_MODEL = 1024
BATCH = 2
SEQ = 8192
DEPTH = 1

MIX_WIDTH = D_MODEL
RET_WIDTH = MIX_WIDTH // 2
CONV_WIDTH = MIX_WIDTH - RET_WIDTH
RET_HEADS = 4
RET_HEAD_DIM = RET_WIDTH // RET_HEADS
CHUNK = 128
CONV_KERNEL = 31
MEM_LEN = 256
XATTN_HEADS = 4
XATTN_HEAD_DIM = D_MODEL // XATTN_HEADS
D_FF = 4 * D_MODEL
ROPE_THETA = 10000.0
EPS = 1e-6
IN_COLS = 4 * RET_WIDTH + 2 * CONV_WIDTH

kernel_name = "hybrid_retention_conformer_xattn_block"


def rmsnorm(x, w):
    xf = x.astype(jnp.float32)
    y = xf * lax.rsqrt(jnp.mean(xf * xf, axis=-1, keepdims=True) + EPS)
    return (y * w.astype(jnp.float32)).astype(x.dtype)


def layernorm(x, w, b):
    xf = x.astype(jnp.float32)
    mu = jnp.mean(xf, axis=-1, keepdims=True)
    var = jnp.mean(jnp.square(xf - mu), axis=-1, keepdims=True)
    y = (xf - mu) * lax.rsqrt(var + EPS)
    return y * w.astype(jnp.float32) + b.astype(jnp.float32)


def rotary(x, positions):
    dh = x.shape[-1]
    half = dh // 2
    inv_freq = ROPE_THETA ** (-jnp.arange(half, dtype=jnp.float32) / half)
    ang = positions.astype(jnp.float32)[..., None] * inv_freq
    cos = jnp.cos(ang)[:, :, None, :]
    sin = jnp.sin(ang)[:, :, None, :]
    xf = x.astype(jnp.float32)
    x1, x2 = xf[..., :half], xf[..., half:]
    return jnp.concatenate([x1 * cos - x2 * sin, x2 * cos + x1 * sin], axis=-1)


def to_chunks(t):
    b, s, h, d = t.shape
    return t.reshape(b, s // CHUNK, CHUNK, h, d).transpose(0, 3, 1, 2, 4)


def from_chunks(t):
    b, h, nc, c, d = t.shape
    return t.transpose(0, 2, 3, 1, 4).reshape(b, nc * c, h, d)


def retention_chunkwise(q, k, v, log_gamma, include_diag):
    c = q.shape[3]
    idx = jnp.arange(c, dtype=jnp.float32)
    diff = idx[:, None] - idx[None, :]
    mask = (diff >= 0) if include_diag else (diff > 0)
    decay_in = jnp.where(mask[None], jnp.exp(log_gamma[:, None, None] * jnp.maximum(diff, 0.0)[None]), 0.0)
    scores = jnp.einsum('bhnqd,bhnkd->bhnqk', q, k) * decay_in[None, :, None]
    inner = jnp.einsum('bhnqk,bhnkd->bhnqd', scores, v)
    zeta = jnp.exp(log_gamma[:, None] * (c - 1 - idx)[None])
    chunk_kv = jnp.einsum('bhnkd,bhnke->bhnde', k * zeta[None, :, None, :, None], v)
    chunk_decay = jnp.exp(log_gamma * c)[None, :, None, None]

    def step(state, kv_n):
        return chunk_decay * state + kv_n, state

    init = jnp.zeros(chunk_kv.shape[:2] + chunk_kv.shape[3:], jnp.float32)
    _, r_prev = lax.scan(step, init, jnp.moveaxis(chunk_kv, 2, 0))
    r_prev = jnp.moveaxis(r_prev, 0, 2)
    xi = jnp.exp(log_gamma[:, None] * (idx + 1.0)[None])
    cross = jnp.einsum('bhnqd,bhnde->bhnqe', q * xi[None, :, None, :, None], r_prev)
    return inner + cross


def retention_group(proj_q, proj_k, proj_v, proj_g, positions, decay_f, decay_b, gn_w, gn_b):
    b, s, _ = proj_q.shape
    q = rotary(proj_q.reshape(b, s, RET_HEADS, RET_HEAD_DIM), positions)
    k = rotary(proj_k.reshape(b, s, RET_HEADS, RET_HEAD_DIM), positions) * (RET_HEAD_DIM ** -0.5)
    v = proj_v.reshape(b, s, RET_HEADS, RET_HEAD_DIM).astype(jnp.float32)
    lg_f = jax.nn.log_sigmoid(decay_f.astype(jnp.float32))
    lg_b = jax.nn.log_sigmoid(decay_b.astype(jnp.float32))
    y_f = from_chunks(retention_chunkwise(to_chunks(q), to_chunks(k), to_chunks(v), lg_f, True))
    y_b = from_chunks(retention_chunkwise(to_chunks(q[:, ::-1]), to_chunks(k[:, ::-1]),
                                          to_chunks(v[:, ::-1]), lg_b, False))[:, ::-1]
    y = y_f + y_b
    mu = jnp.mean(y, axis=-1, keepdims=True)
    var = jnp.mean(jnp.square(y - mu), axis=-1, keepdims=True)
    y = ((y - mu) * lax.rsqrt(var + EPS)).reshape(b, s, RET_WIDTH)
    y = y * gn_w.astype(jnp.float32) + gn_b.astype(jnp.float32)
    return (jax.nn.silu(proj_g.astype(jnp.float32)) * y).astype(proj_q.dtype)


def conformer_conv_group(proj_a, proj_b, conv_w, conv_b, ln_w, ln_b):
    u = proj_a * jax.nn.sigmoid(proj_b)
    pad = CONV_KERNEL // 2
    y = lax.conv_general_dilated(u, conv_w, window_strides=(1,), padding=((pad, pad),),
                                 dimension_numbers=('NWC', 'WIO', 'NWC'),
                                 feature_group_count=CONV_WIDTH) + conv_b
    y = layernorm(y, ln_w, ln_b)
    return jax.nn.silu(y).astype(proj_a.dtype)


def memory_cross_attention(h, m, w_q, w_kv, w_o):
    b, s, _ = h.shape
    q = (h @ w_q).reshape(b, s, XATTN_HEADS, XATTN_HEAD_DIM)
    kv = m @ w_kv
    k = kv[..., :D_MODEL].reshape(b, m.shape[1], XATTN_HEADS, XATTN_HEAD_DIM)
    v = kv[..., D_MODEL:].reshape(b, m.shape[1], XATTN_HEADS, XATTN_HEAD_DIM)
    scores = jnp.einsum('bshd,bmhd->bhsm', q.astype(jnp.float32), k.astype(jnp.float32)) * (XATTN_HEAD_DIM ** -0.5)
    p = jax.nn.softmax(scores, axis=-1)
    o = jnp.einsum('bhsm,bmhd->bshd', p, v.astype(jnp.float32)).reshape(b, s, D_MODEL)
    return o.astype(h.dtype) @ w_o


def setup_inputs(seed: int = 0) -> dict:
    key = jax.random.key(seed)
    ks = jax.random.split(key, 24)
    f32 = jnp.float32

    def nrm(k, shape, scale):
        return jax.random.normal(k, shape, f32) * scale

    def gain(k, shape):
        return 1.0 + 0.05 * jax.random.normal(k, shape, f32)

    g0 = 1.0 - 2.0 ** (-5.0 - np.arange(RET_HEADS, dtype=np.float32))
    base_logit = jnp.asarray(np.log(g0) - np.log1p(-g0), f32)
    return {
        "x": jax.random.normal(ks[0], (BATCH, SEQ, D_MODEL), f32),
        "mem": jax.random.normal(ks[1], (BATCH, MEM_LEN, D_MODEL), f32),
        "positions": jnp.broadcast_to(jnp.arange(SEQ, dtype=jnp.int32), (BATCH, SEQ)),
        "mix_norm_w": gain(ks[2], (DEPTH, D_MODEL)),
        "w_in": nrm(ks[3], (DEPTH, D_MODEL, IN_COLS), D_MODEL ** -0.5),
        "ret_decay_f": base_logit[None] + 0.1 * jax.random.normal(ks[4], (DEPTH, RET_HEADS), f32),
        "ret_decay_b": base_logit[None] + 0.1 * jax.random.normal(ks[5], (DEPTH, RET_HEADS), f32),
        "ret_gn_w": gain(ks[6], (DEPTH, RET_WIDTH)),
        "ret_gn_b": nrm(ks[7], (DEPTH, RET_WIDTH), 0.02),
        "conv_w": nrm(ks[8], (DEPTH, CONV_KERNEL, 1, CONV_WIDTH), CONV_KERNEL ** -0.5),
        "conv_b": nrm(ks[9], (DEPTH, CONV_WIDTH), 0.02),
        "conv_ln_w": gain(ks[10], (DEPTH, CONV_WIDTH)),
        "conv_ln_b": nrm(ks[11], (DEPTH, CONV_WIDTH), 0.02),
        "w_out": nrm(ks[12], (DEPTH, MIX_WIDTH, D_MODEL), MIX_WIDTH ** -0.5),
        "xattn_norm_w": gain(ks[13], (DEPTH, D_MODEL)),
        "mem_norm_w": gain(ks[14], (DEPTH, D_MODEL)),
        "w_xq": nrm(ks[15], (DEPTH, D_MODEL, D_MODEL), D_MODEL ** -0.5),
        "w_xkv": nrm(ks[16], (DEPTH, D_MODEL, 2 * D_MODEL), D_MODEL ** -0.5),
        "w_xo": nrm(ks[17], (DEPTH, D_MODEL, D_MODEL), D_MODEL ** -0.5),
        "mlp_norm_w": gain(ks[18], (DEPTH, D_MODEL)),
        "w_ff1": nrm(ks[19], (DEPTH, D_MODEL, D_FF), D_MODEL ** -0.5),
        "w_ff2": nrm(ks[20], (DEPTH, D_FF, D_MODEL), D_FF ** -0.5),
        "final_norm_w": gain(ks[21], (D_MODEL,)),
    }


def reference(x, mem, positions, mix_norm_w, w_in, ret_decay_f, ret_decay_b, ret_gn_w, ret_gn_b,
              conv_w, conv_b, conv_ln_w, conv_ln_b, w_out, xattn_norm_w, mem_norm_w,
              w_xq, w_xkv, w_xo, mlp_norm_w, w_ff1, w_ff2, final_norm_w):
    R = RET_WIDTH
    for l in range(DEPTH):
        h = rmsnorm(x, mix_norm_w[l])
        proj = h @ w_in[l]
        p_q, p_k, p_v, p_g = proj[..., :R], proj[..., R:2 * R], proj[..., 2 * R:3 * R], proj[..., 3 * R:4 * R]
        p_a = proj[..., 4 * R:4 * R + CONV_WIDTH]
        p_b = proj[..., 4 * R + CONV_WIDTH:]
        y_ret = retention_group(p_q, p_k, p_v, p_g, positions, ret_decay_f[l], ret_decay_b[l],
                                ret_gn_w[l], ret_gn_b[l])
        y_conv = conformer_conv_group(p_a, p_b, conv_w[l], conv_b[l], conv_ln_w[l], conv_ln_b[l])
        x = x + jnp.concatenate([y_ret, y_conv], axis=-1) @ w_out[l]
        x = x + memory_cross_attention(rmsnorm(x, xattn_norm_w[l]), rmsnorm(mem, mem_norm_w[l]),
                                       w_xq[l], w_xkv[l], w_xo[l])
        hm = rmsnorm(x, mlp_norm_w[l])
        x = x + jnp.square(jax.nn.relu(hm @ w_ff1[l])) @ w_ff2[l]
    return rmsnorm(x, final_norm_w)
```

```python
import functools

import jax
import jax.numpy as jnp
import numpy as np
from jax import lax
from jax.experimental import pallas as pl
from jax.experimental.pallas import tpu as pltpu

F32 = jnp.float32
BF16 = jnp.bfloat16

D_MODEL = 1024
RET_WIDTH = 512
CONV_WIDTH = 512
RET_HEADS = 4
HEAD_DIM = 128
CHUNK = 128
CONV_KERNEL = 31
CONV_PAD = CONV_KERNEL // 2
HALO = 16
XATTN_HEADS = 4
XATTN_HEAD_DIM = 256
D_FF = 4096
ROPE_THETA = 10000.0
EPS = 1e-6

IN_PROJ_TILE = 1024
REST_TILE = 512
FF_CHUNK = 1024
CONV_ROWS = 32
VMEM_LIMIT_BYTES = 58 * 1024 * 1024


def _rms(x, w):
    ms = jnp.mean(x * x, axis=-1, keepdims=True)
    return x * lax.rsqrt(ms + EPS) * w


def _dot(a, b):
    return jnp.dot(a, b, preferred_element_type=F32)


def _dot_tn(a, b):
    return lax.dot_general(a, b, (((0,), (0,)), ((), ())), preferred_element_type=F32)


def _dot_nt(a, b):
    return lax.dot_general(a, b, (((1,), (1,)), ((), ())), preferred_element_type=F32)


def _const_spec(shape):
    nd = len(shape)
    return pl.BlockSpec(shape, lambda *_: (0,) * nd, pipeline_mode=pl.Buffered(1))


def _mem_kv_body(mem_ref, nw_ref, wkv_ref, kt_ref, v_ref):
    m = _rms(mem_ref[0], nw_ref[...]).astype(BF16)
    kv = _dot(m, wkv_ref[...])
    k = kv[:, :D_MODEL] * (XATTN_HEAD_DIM ** -0.5)
    kt_ref[0] = k.T.astype(BF16)
    v_ref[0] = kv[:, D_MODEL:].astype(BF16)


def _mem_kv(mem, mem_norm_w, w_xkv):
    b, m, d = mem.shape
    return pl.pallas_call(
        _mem_kv_body,
        grid=(b,),
        in_specs=[pl.BlockSpec((1, m, d), lambda i: (i, 0, 0)),
                  _const_spec((1, d)),
                  _const_spec((d, 2 * d))],
        out_specs=[pl.BlockSpec((1, d, m), lambda i: (i, 0, 0)),
                   pl.BlockSpec((1, m, d), lambda i: (i, 0, 0))],
        out_shape=[jax.ShapeDtypeStruct((b, d, m), BF16),
                   jax.ShapeDtypeStruct((b, m, d), BF16)],
        compiler_params=pltpu.CompilerParams(
            dimension_semantics=("arbitrary",), vmem_limit_bytes=VMEM_LIMIT_BYTES),
        name="mem_kv",
    )(mem, mem_norm_w, w_xkv)


def _in_proj_body(x_ref, pos_ref, nw_ref, invf_ref, sign_ref, lgf_ref, w_ref,
                  q_ref, k_ref, v_ref, sg_ref, u_ref, rf_ref, state_ref):
    t = x_ref.shape[1]
    R = RET_WIDTH

    @pl.when(pl.program_id(1) == 0)
    def _():
        state_ref[...] = jnp.zeros_like(state_ref)

    h = _rms(x_ref[0], nw_ref[...]).astype(BF16)

    ang = pos_ref[0].astype(F32) * invf_ref[...]
    cos2 = jnp.cos(ang)
    sin2 = jnp.sin(ang) * sign_ref[...]

    def rope(p):
        outs = []
        for hd in range(RET_HEADS):
            ph = p[:, hd * HEAD_DIM:(hd + 1) * HEAD_DIM]
            outs.append(ph * cos2 + pltpu.roll(ph, HEAD_DIM // 2, 1) * sin2)
        return jnp.concatenate(outs, axis=1)

    q = rope(_dot(h, w_ref[:, 0:R]))
    q_ref[0] = q.astype(BF16)
    k = rope(_dot(h, w_ref[:, R:2 * R])) * (HEAD_DIM ** -0.5)
    k_ref[0] = k.astype(BF16)
    v = _dot(h, w_ref[:, 2 * R:3 * R]).astype(BF16)
    v_ref[0] = v
    g = _dot(h, w_ref[:, 3 * R:4 * R])
    sg_ref[0] = (g * jax.nn.sigmoid(g)).astype(BF16)
    a = _dot(h, w_ref[:, 4 * R:5 * R])
    bb = _dot(h, w_ref[:, 5 * R:6 * R])
    u_ref[0] = (a * jax.nn.sigmoid(bb)).astype(BF16)

    lgf = lgf_ref[...]
    row = lax.broadcasted_iota(jnp.int32, (CHUNK, R), 0).astype(F32)
    zeta = jnp.exp(lgf * (CHUNK - 1.0 - row))
    chunk_decay = jnp.exp(lgf * float(CHUNK))
    state = state_ref[...]
    for c in range(t // CHUNK):
        rf_ref[0, c] = state.astype(BF16)
        rows = slice(c * CHUNK, (c + 1) * CHUNK)
        kz = (k[rows] * zeta).astype(BF16)
        vc = v[rows]
        upd = jnp.concatenate(
            [_dot_tn(kz[:, hd * HEAD_DIM:(hd + 1) * HEAD_DIM],
                     vc[:, hd * HEAD_DIM:(hd + 1) * HEAD_DIM]) for hd in range(RET_HEADS)], axis=1)
        state = chunk_decay * state + upd
    state_ref[...] = state


def _in_proj(x, pos3, mix_norm_w, invf2, sign2, lgf_lanes, w_in):
    b, s, d = x.shape
    t = IN_PROJ_TILE
    R = RET_WIDTH
    tok = lambda i, j: (i, j, 0)
    act_spec = pl.BlockSpec((1, t, R), tok)
    act_shape = jax.ShapeDtypeStruct((b, s, R), BF16)
    return pl.pallas_call(
        _in_proj_body,
        grid=(b, s // t),
        in_specs=[pl.BlockSpec((1, t, d), tok),
                  pl.BlockSpec((1, t, 1), tok),
                  _const_spec((1, d)),
                  _const_spec((1, HEAD_DIM)),
                  _const_spec((1, HEAD_DIM)),
                  _const_spec((1, R)),
                  _const_spec((d, 6 * R))],
        out_specs=[act_spec] * 5 + [pl.BlockSpec((1, t // CHUNK, CHUNK, R), lambda i, j: (i, j, 0, 0))],
        out_shape=[act_shape] * 5 + [jax.ShapeDtypeStruct((b, s // CHUNK, CHUNK, R), BF16)],
        scratch_shapes=[pltpu.VMEM((CHUNK, R), F32)],
        compiler_params=pltpu.CompilerParams(
            dimension_semantics=("arbitrary", "arbitrary"), vmem_limit_bytes=VMEM_LIMIT_BYTES),
        name="in_proj",
    )(x, pos3, mix_norm_w, invf2, sign2, lgf_lanes, w_in)


def _rest_body(x_ref, q_ref, k_ref, v_ref, sg_ref, u_ref, up_ref, un_ref, rf_ref,
               lgf_ref, lgb_ref, gnw_ref, gnb_ref, cw_ref, cb_ref, lnw_ref, lnb_ref,
               wout_ref, xnw_ref, kt_ref, vm_ref, wxq_ref, wxo_ref,
               mnw_ref, w1_ref, w2_ref, fnw_ref,
               o_ref, rb_ref, mix_ref, uext_ref, conv_ref):
    t = x_ref.shape[1]
    R = RET_WIDTH
    j = pl.program_id(1)
    n_tiles = pl.num_programs(1)
    tile = n_tiles - 1 - j

    @pl.when(j == 0)
    def _():
        rb_ref[...] = jnp.zeros_like(rb_ref)

    lgf = lgf_ref[...]
    lgb = lgb_ref[...]
    row = lax.broadcasted_iota(jnp.int32, (CHUNK, R), 0).astype(F32)
    xi_f = jnp.exp(lgf * (row + 1.0))
    xi_b = jnp.exp(lgb * (float(CHUNK) - row))
    zeta_b = jnp.exp(lgb * row)
    decay_b = jnp.exp(lgb * float(CHUNK))
    ii = lax.broadcasted_iota(jnp.int32, (CHUNK, CHUNK), 0)
    jj = lax.broadcasted_iota(jnp.int32, (CHUNK, CHUNK), 1)
    diff = (ii - jj).astype(F32)
    dmask = []
    for hd in range(RET_HEADS):
        hs = slice(hd * HEAD_DIM, (hd + 1) * HEAD_DIM)
        dmask.append(jnp.where(diff >= 0.0,
                               jnp.exp(lgf[:, hs] * jnp.maximum(diff, 0.0)),
                               jnp.exp(lgb[:, hs] * jnp.maximum(-diff, 0.0))))

    gnw = gnw_ref[...]
    gnb = gnb_ref[...]
    rb = rb_ref[...]
    for c in reversed(range(t // CHUNK)):
        rows = slice(c * CHUNK, (c + 1) * CHUNK)
        qc = q_ref[0, rows, :]
        kc = k_ref[0, rows, :]
        vc = v_ref[0, rows, :]
        sgc = sg_ref[0, rows, :].astype(F32)
        rfc = rf_ref[0, c]
        rbb = rb.astype(BF16)
        kzb = (kc.astype(F32) * zeta_b).astype(BF16)
        ys = []
        upds = []
        for hd in range(RET_HEADS):
            hs = slice(hd * HEAD_DIM, (hd + 1) * HEAD_DIM)
            qh, kh, vh = qc[:, hs], kc[:, hs], vc[:, hs]
            p = (_dot_nt(qh, kh) * dmask[hd]).astype(BF16)
            inner = _dot(p, vh)
            cr = _dot(qh, jnp.concatenate([rfc[:, hs], rbb[:, hs]], axis=1))
            y = inner + cr[:, :HEAD_DIM] * xi_f[:, hs] + cr[:, HEAD_DIM:] * xi_b[:, hs]
            mu = jnp.mean(y, axis=-1, keepdims=True)
            yc = y - mu
            var = jnp.mean(yc * yc, axis=-1, keepdims=True)
            ys.append(yc * lax.rsqrt(var + EPS))
            upds.append(_dot_tn(kzb[:, hs], vh))
        yn = jnp.concatenate(ys, axis=1) * gnw + gnb
        mix_ref[rows, 0:R] = (sgc * yn).astype(BF16)
        rb = decay_b * rb + jnp.concatenate(upds, axis=1)
    rb_ref[...] = rb

    zero_halo = jnp.zeros((HALO, CONV_WIDTH), F32)
    uext_ref[0:HALO, :] = jnp.where(tile > 0, up_ref[0].astype(F32), zero_halo)
    uext_ref[HALO:HALO + t, :] = u_ref[0].astype(F32)
    uext_ref[HALO + t:HALO + t + HALO, :] = jnp.where(tile < n_tiles - 1, un_ref[0].astype(F32), zero_halo)
    cb = cb_ref[...]

    for base in range(0, t, CONV_ROWS):
        acc = jnp.broadcast_to(cb, (CONV_ROWS, CONV_WIDTH))
        for tap in range(CONV_KERNEL):
            off = base + HALO - CONV_PAD + tap
            acc = acc + cw_ref[tap:tap + 1, :] * uext_ref[off:off + CONV_ROWS, :]
        conv_ref[base:base + CONV_ROWS, :] = acc
    yc = conv_ref[...]
    mu = jnp.mean(yc, axis=-1, keepdims=True)
    ycc = yc - mu
    var = jnp.mean(ycc * ycc, axis=-1, keepdims=True)
    yl = ycc * lax.rsqrt(var + EPS) * lnw_ref[...] + lnb_ref[...]
    mix_ref[:, R:R + CONV_WIDTH] = (yl * jax.nn.sigmoid(yl)).astype(BF16)

    x1 = x_ref[0] + _dot(mix_ref[...], wout_ref[...])

    h2 = _rms(x1, xnw_ref[...]).astype(BF16)
    qx = _dot(h2, wxq_ref[...]).astype(BF16)
    outs = []
    for hd in range(XATTN_HEADS):
        hs = slice(hd * XATTN_HEAD_DIM, (hd + 1) * XATTN_HEAD_DIM)
        s = _dot(qx[:, hs], kt_ref[0, hs, :])
        e = jnp.exp(s - jnp.max(s, axis=-1, keepdims=True))
        p = e * (1.0 / jnp.sum(e, axis=-1, keepdims=True))
        outs.append(_dot(p.astype(BF16), vm_ref[0, :, hs]).astype(BF16))
    x2 = x1 + _dot(jnp.concatenate(outs, axis=1), wxo_ref[...])

    hm = _rms(x2, mnw_ref[...]).astype(BF16)
    acc = jnp.zeros((t, D_MODEL), F32)
    for c in range(D_FF // FF_CHUNK):
        cs = slice(c * FF_CHUNK, (c + 1) * FF_CHUNK)
        hc = jnp.maximum(_dot(hm, w1_ref[:, cs]), 0.0)
        acc = acc + _dot((hc * hc).astype(BF16), w2_ref[cs, :])
    x3 = x2 + acc
    o_ref[0] = _rms(x3, fnw_ref[...])


def _block_rest(x, q, k, v, sg, u, rf, lgf_lanes, lgb_lanes, gn_w, gn_b, conv_w, conv_b, ln_w, ln_b,
                w_out, xattn_norm_w, kt, vm, w_xq, w_xo, mlp_norm_w, w_ff1, w_ff2, final_norm_w):
    b, s, d = x.shape
    t = REST_TILE
    R = RET_WIDTH
    n_tiles = s // t
    hb = t // HALO
    n_hb = s // HALO
    rev = lambda i, j: (i, n_tiles - 1 - j, 0)
    prev_halo = lambda i, j: (i, jnp.maximum((n_tiles - 1 - j) * hb - 1, 0), 0)
    next_halo = lambda i, j: (i, jnp.minimum((n_tiles - j) * hb, n_hb - 1), 0)
    act_spec = pl.BlockSpec((1, t, R), rev)
    halo_shape = (1, HALO, R)
    m = kt.shape[2]
    return pl.pallas_call(
        _rest_body,
        grid=(b, n_tiles),
        in_specs=[pl.BlockSpec((1, t, d), rev),
                  act_spec, act_spec, act_spec, act_spec, act_spec,
                  pl.BlockSpec(halo_shape, prev_halo),
                  pl.BlockSpec(halo_shape, next_halo),
                  pl.BlockSpec((1, t // CHUNK, CHUNK, R), lambda i, j: (i, n_tiles - 1 - j, 0, 0)),
                  _const_spec((1, R)), _const_spec((1, R)),
                  _const_spec((1, R)), _const_spec((1, R)),
                  _const_spec((CONV_KERNEL, CONV_WIDTH)),
                  _const_spec((1, CONV_WIDTH)), _const_spec((1, CONV_WIDTH)), _const_spec((1, CONV_WIDTH)),
                  _const_spec((d, d)),
                  _const_spec((1, d)),
                  pl.BlockSpec((1, d, m), lambda i, j: (i, 0, 0)),
                  pl.BlockSpec((1, m, d), lambda i, j: (i, 0, 0)),
                  _const_spec((d, d)), _const_spec((d, d)),
                  _const_spec((1, d)),
                  _const_spec((d, D_FF)), _const_spec((D_FF, d)),
                  _const_spec((1, d))],
        out_specs=pl.BlockSpec((1, t, d), rev),
        out_shape=jax.ShapeDtypeStruct((b, s, d), F32),
        scratch_shapes=[pltpu.VMEM((CHUNK, R), F32),
                        pltpu.VMEM((t, D_MODEL), BF16),
                        pltpu.VMEM((t + 2 * HALO, CONV_WIDTH), F32),
                        pltpu.VMEM((t, CONV_WIDTH), F32)],
        compiler_params=pltpu.CompilerParams(
            dimension_semantics=("arbitrary", "arbitrary"), vmem_limit_bytes=VMEM_LIMIT_BYTES),
        name="block_rest",
    )(x, q, k, v, sg, u, u, u, rf, lgf_lanes, lgb_lanes, gn_w, gn_b, conv_w, conv_b, ln_w, ln_b,
      w_out, xattn_norm_w, kt, vm, w_xq, w_xo, mlp_norm_w, w_ff1, w_ff2, final_norm_w)


def kernel(x, mem, positions, mix_norm_w, w_in, ret_decay_f, ret_decay_b, ret_gn_w, ret_gn_b,
           conv_w, conv_b, conv_ln_w, conv_ln_b, w_out, xattn_norm_w, mem_norm_w,
           w_xq, w_xkv, w_xo, mlp_norm_w, w_ff1, w_ff2, final_norm_w):
    assert w_in.shape[0] == 1, "single-layer block"
    b, s, d = x.shape
    half = HEAD_DIM // 2
    inv_freq = ROPE_THETA ** (-jnp.arange(half, dtype=F32) / half)
    invf2 = jnp.concatenate([inv_freq, inv_freq])[None]
    sign2 = jnp.concatenate([-jnp.ones((half,), F32), jnp.ones((half,), F32)])[None]
    pos3 = positions.reshape(b, s, 1)
    row = lambda a: a.reshape(1, -1)
    lgf_lanes = jnp.repeat(jax.nn.log_sigmoid(ret_decay_f[0].astype(F32)), HEAD_DIM)[None]
    lgb_lanes = jnp.repeat(jax.nn.log_sigmoid(ret_decay_b[0].astype(F32)), HEAD_DIM)[None]
    kt, vm = _mem_kv(mem, row(mem_norm_w[0]), w_xkv[0].astype(BF16))
    q, k, v, sg, u, rf = _in_proj(x, pos3, row(mix_norm_w[0]), invf2, sign2, lgf_lanes,
                                  w_in[0].astype(BF16))
    return _block_rest(x, q, k, v, sg, u, rf, lgf_lanes, lgb_lanes,
                       row(ret_gn_w[0]), row(ret_gn_b[0]),
                       conv_w[0].reshape(CONV_KERNEL, CONV_WIDTH), row(conv_b[0]),
                       row(conv_ln_w[0]), row(conv_ln_b[0]),
                       w_out[0].astype(BF16), row(xattn_norm_w[0]), kt, vm,
                       w_xq[0].astype(BF16), w_xo[0].astype(BF16),
                       row(mlp_norm_w[0]), w_ff1[0].astype(BF16), w_ff2[0].astype(BF16),
                       row(final_norm_w))
```

```python
import functools

import jax
import jax.numpy as jnp
import numpy as np
from jax import lax
from jax.experimental import pallas as pl
from jax.experimental.pallas import tpu as pltpu

F32 = jnp.float32
BF16 = jnp.bfloat16

D_MODEL = 1024
RET_WIDTH = 512
CONV_WIDTH = 512
RET_HEADS = 4
HEAD_DIM = 128
CHUNK = 128
CONV_KERNEL = 31
CONV_PAD = CONV_KERNEL // 2
HALO = 16
XATTN_HEADS = 4
XATTN_HEAD_DIM = 256
D_FF = 4096
ROPE_THETA = 10000.0
EPS = 1e-6

IN_PROJ_TILE = 1024
REST_TILE = 512
FF_CHUNK = 1024
CONV_ROWS = 32
CONV_SPAN = 256
SUBLANES = 8
SHIFT_ROWS = ((HALO - CONV_PAD + CONV_KERNEL - 1) // SUBLANES) * SUBLANES
VMEM_LIMIT_BYTES = 58 * 1024 * 1024


def _rms(x, w):
    ms = jnp.mean(x * x, axis=-1, keepdims=True)
    return x * lax.rsqrt(ms + EPS) * w


def _dot(a, b):
    return jnp.dot(a, b, preferred_element_type=F32)


def _dot_tn(a, b):
    return lax.dot_general(a, b, (((0,), (0,)), ((), ())), preferred_element_type=F32)


def _dot_nt(a, b):
    return lax.dot_general(a, b, (((1,), (1,)), ((), ())), preferred_element_type=F32)


def _const_spec(shape):
    nd = len(shape)
    return pl.BlockSpec(shape, lambda *_: (0,) * nd, pipeline_mode=pl.Buffered(1))


def _mem_kv_body(mem_ref, nw_ref, wkv_ref, kt_ref, v_ref):
    m = _rms(mem_ref[0], nw_ref[...]).astype(BF16)
    kv = _dot(m, wkv_ref[...])
    k = kv[:, :D_MODEL] * (XATTN_HEAD_DIM ** -0.5)
    kt_ref[0] = k.T.astype(BF16)
    v_ref[0] = kv[:, D_MODEL:].astype(BF16)


def _mem_kv(mem, mem_norm_w, w_xkv):
    b, m, d = mem.shape
    return pl.pallas_call(
        _mem_kv_body,
        grid=(b,),
        in_specs=[pl.BlockSpec((1, m, d), lambda i: (i, 0, 0)),
                  _const_spec((1, d)),
                  _const_spec((d, 2 * d))],
        out_specs=[pl.BlockSpec((1, d, m), lambda i: (i, 0, 0)),
                   pl.BlockSpec((1, m, d), lambda i: (i, 0, 0))],
        out_shape=[jax.ShapeDtypeStruct((b, d, m), BF16),
                   jax.ShapeDtypeStruct((b, m, d), BF16)],
        compiler_params=pltpu.CompilerParams(
            dimension_semantics=("arbitrary",), vmem_limit_bytes=VMEM_LIMIT_BYTES),
        name="mem_kv",
    )(mem, mem_norm_w, w_xkv)


def _in_proj_body(x_ref, pos_ref, nw_ref, invf_ref, sign_ref, lgf_ref, w_ref,
                  q_ref, k_ref, v_ref, sg_ref, u_ref, rf_ref, state_ref):
    t = x_ref.shape[1]
    R = RET_WIDTH

    @pl.when(pl.program_id(1) == 0)
    def _():
        state_ref[...] = jnp.zeros_like(state_ref)

    h = _rms(x_ref[0], nw_ref[...]).astype(BF16)

    ang = pos_ref[0].astype(F32) * invf_ref[...]
    cos2 = jnp.cos(ang)
    sin2 = jnp.sin(ang) * sign_ref[...]

    def rope(p):
        outs = []
        for hd in range(RET_HEADS):
            ph = p[:, hd * HEAD_DIM:(hd + 1) * HEAD_DIM]
            outs.append(ph * cos2 + pltpu.roll(ph, HEAD_DIM // 2, 1) * sin2)
        return jnp.concatenate(outs, axis=1)

    q = rope(_dot(h, w_ref[:, 0:R]))
    q_ref[0] = q.astype(BF16)
    k = rope(_dot(h, w_ref[:, R:2 * R])) * (HEAD_DIM ** -0.5)
    k_ref[0] = k.astype(BF16)
    v = _dot(h, w_ref[:, 2 * R:3 * R]).astype(BF16)
    v_ref[0] = v
    g = _dot(h, w_ref[:, 3 * R:4 * R])
    sg_ref[0] = (g * jax.nn.sigmoid(g)).astype(BF16)
    a = _dot(h, w_ref[:, 4 * R:5 * R])
    bb = _dot(h, w_ref[:, 5 * R:6 * R])
    u_ref[0] = (a * jax.nn.sigmoid(bb)).astype(BF16)

    lgf = lgf_ref[...]
    row = lax.broadcasted_iota(jnp.int32, (CHUNK, R), 0).astype(F32)
    zeta = jnp.exp(lgf * (CHUNK - 1.0 - row))
    chunk_decay = jnp.exp(lgf * float(CHUNK))
    state = state_ref[...]
    for c in range(t // CHUNK):
        rf_ref[0, c] = state.astype(BF16)
        rows = slice(c * CHUNK, (c + 1) * CHUNK)
        kz = (k[rows] * zeta).astype(BF16)
        vc = v[rows]
        upd = jnp.concatenate(
            [_dot_tn(kz[:, hd * HEAD_DIM:(hd + 1) * HEAD_DIM],
                     vc[:, hd * HEAD_DIM:(hd + 1) * HEAD_DIM]) for hd in range(RET_HEADS)], axis=1)
        state = chunk_decay * state + upd
    state_ref[...] = state


def _in_proj(x, pos3, mix_norm_w, invf2, sign2, lgf_lanes, w_in):
    b, s, d = x.shape
    t = IN_PROJ_TILE
    R = RET_WIDTH
    tok = lambda i, j: (i, j, 0)
    act_spec = pl.BlockSpec((1, t, R), tok)
    act_shape = jax.ShapeDtypeStruct((b, s, R), BF16)
    return pl.pallas_call(
        _in_proj_body,
        grid=(b, s // t),
        in_specs=[pl.BlockSpec((1, t, d), tok),
                  pl.BlockSpec((1, t, 1), tok),
                  _const_spec((1, d)),
                  _const_spec((1, HEAD_DIM)),
                  _const_spec((1, HEAD_DIM)),
                  _const_spec((1, R)),
                  _const_spec((d, 6 * R))],
        out_specs=[act_spec] * 5 + [pl.BlockSpec((1, t // CHUNK, CHUNK, R), lambda i, j: (i, j, 0, 0))],
        out_shape=[act_shape] * 5 + [jax.ShapeDtypeStruct((b, s // CHUNK, CHUNK, R), BF16)],
        scratch_shapes=[pltpu.VMEM((CHUNK, R), F32)],
        compiler_params=pltpu.CompilerParams(
            dimension_semantics=("arbitrary", "arbitrary"), vmem_limit_bytes=VMEM_LIMIT_BYTES),
        name="in_proj",
    )(x, pos3, mix_norm_w, invf2, sign2, lgf_lanes, w_in)


def _conv_pieces(u_ref, up_ref, un_ref, has_prev, has_next,
                 cw_ref, cb_ref, lnw_ref, lnb_ref, uext_ref, ush_ref, yconv_ref):
    t = u_ref.shape[1]
    span = CONV_SPAN + SHIFT_ROWS

    def fill():
        zero_halo = jnp.zeros((HALO, CONV_WIDTH), F32)
        uext_ref[0:HALO, :] = jnp.where(has_prev, up_ref[0].astype(F32), zero_halo)
        uext_ref[HALO:HALO + t, :] = u_ref[0].astype(F32)
        uext_ref[HALO + t:HALO + t + HALO, :] = jnp.where(has_next, un_ref[0].astype(F32), zero_halo)

    def shift(hb):
        for sft in range(1, SUBLANES):
            ush_ref[sft, 0:span, :] = uext_ref[hb + sft:hb + sft + span, :]

    def block(hb, base):
        acc = jnp.broadcast_to(cb_ref[...], (CONV_ROWS, CONV_WIDTH))
        for tap in range(CONV_KERNEL):
            grp, sft = divmod(HALO - CONV_PAD + tap, SUBLANES)
            lo = base + grp * SUBLANES
            if sft == 0:
                src = uext_ref[hb + lo:hb + lo + CONV_ROWS, :]
            else:
                src = ush_ref[sft, lo:lo + CONV_ROWS, :]
            acc = acc + cw_ref[tap:tap + 1, :] * src
        mu = jnp.mean(acc, axis=-1, keepdims=True)
        yc = acc - mu
        var = jnp.mean(yc * yc, axis=-1, keepdims=True)
        yl = yc * lax.rsqrt(var + EPS) * lnw_ref[...] + lnb_ref[...]
        yconv_ref[hb + base:hb + base + CONV_ROWS, :] = (yl * jax.nn.sigmoid(yl)).astype(BF16)

    pieces = []
    for hb in range(0, t, CONV_SPAN):
        for base in range(0, CONV_SPAN, CONV_ROWS):
            def piece(hb=hb, base=base):
                if hb == 0 and base == 0:
                    fill()
                if base == 0:
                    shift(hb)
                block(hb, base)
            pieces.append(piece)
    return pieces


def _rest_body(n_tiles,
               x_ref, q_ref, k_ref, v_ref, sg_ref, u_ref, up_ref, un_ref, u1_ref, up1_ref, un1_ref, rf_ref,
               lgf_ref, lgb_ref, gnw_ref, gnb_ref, cw_ref, cb_ref, lnw_ref, lnb_ref,
               wout_ref, xnw_ref, kt_ref, vm_ref, wxq_ref, wxo_ref,
               mnw_ref, w1_ref, w2_ref, fnw_ref,
               o_ref, rb_ref, mix_ref, uext_ref, ush_ref, yconv_ref):
    t = x_ref.shape[1]
    R = RET_WIDTH
    step = pl.program_id(0)
    last_step = pl.num_programs(0) - 1
    tile = n_tiles - 1 - step % n_tiles
    step1 = jnp.minimum(step + 1, last_step)
    tile1 = n_tiles - 1 - step1 % n_tiles
    conv_consts = (cw_ref, cb_ref, lnw_ref, lnb_ref, uext_ref, ush_ref, yconv_ref)

    @pl.when(step % n_tiles == 0)
    def _():
        rb_ref[...] = jnp.zeros_like(rb_ref)

    @pl.when(step == 0)
    def _():
        for piece in _conv_pieces(u_ref, up_ref, un_ref, tile > 0, tile < n_tiles - 1, *conv_consts):
            piece()

    mix_ref[:, R:R + CONV_WIDTH] = yconv_ref[...]

    lgf = lgf_ref[...]
    lgb = lgb_ref[...]
    row = lax.broadcasted_iota(jnp.int32, (CHUNK, R), 0).astype(F32)
    xi_f = jnp.exp(lgf * (row + 1.0))
    xi_b = jnp.exp(lgb * (float(CHUNK) - row))
    zeta_b = jnp.exp(lgb * row)
    decay_b = jnp.exp(lgb * float(CHUNK))
    ii = lax.broadcasted_iota(jnp.int32, (CHUNK, CHUNK), 0)
    jj = lax.broadcasted_iota(jnp.int32, (CHUNK, CHUNK), 1)
    diff = (ii - jj).astype(F32)
    dmask = []
    for hd in range(RET_HEADS):
        hs = slice(hd * HEAD_DIM, (hd + 1) * HEAD_DIM)
        dmask.append(jnp.where(diff >= 0.0,
                               jnp.exp(lgf[:, hs] * jnp.maximum(diff, 0.0)),
                               jnp.exp(lgb[:, hs] * jnp.maximum(-diff, 0.0))))

    gnw = gnw_ref[...]
    gnb = gnb_ref[...]
    rb = rb_ref[...]
    for c in reversed(range(t // CHUNK)):
        rows = slice(c * CHUNK, (c + 1) * CHUNK)
        qc = q_ref[0, rows, :]
        kc = k_ref[0, rows, :]
        vc = v_ref[0, rows, :]
        sgc = sg_ref[0, rows, :].astype(F32)
        rfc = rf_ref[0, c]
        rbb = rb.astype(BF16)
        kzb = (kc.astype(F32) * zeta_b).astype(BF16)
        ys = []
        upds = []
        for hd in range(RET_HEADS):
            hs = slice(hd * HEAD_DIM, (hd + 1) * HEAD_DIM)
            qh, kh, vh = qc[:, hs], kc[:, hs], vc[:, hs]
            p = (_dot_nt(qh, kh) * dmask[hd]).astype(BF16)
            inner = _dot(p, vh)
            cr = _dot(qh, jnp.concatenate([rfc[:, hs], rbb[:, hs]], axis=1))
            y = inner + cr[:, :HEAD_DIM] * xi_f[:, hs] + cr[:, HEAD_DIM:] * xi_b[:, hs]
            mu = jnp.mean(y, axis=-1, keepdims=True)
            yc = y - mu
            var = jnp.mean(yc * yc, axis=-1, keepdims=True)
            ys.append(yc * lax.rsqrt(var + EPS))
            upds.append(_dot_tn(kzb[:, hs], vh))
        yn = jnp.concatenate(ys, axis=1) * gnw + gnb
        mix_ref[rows, 0:R] = (sgc * yn).astype(BF16)
        rb = decay_b * rb + jnp.concatenate(upds, axis=1)
    rb_ref[...] = rb

    x1 = x_ref[0] + _dot(mix_ref[...], wout_ref[...])

    h2 = _rms(x1, xnw_ref[...]).astype(BF16)
    qx = _dot(h2, wxq_ref[...]).astype(BF16)
    outs = []
    for hd in range(XATTN_HEADS):
        hs = slice(hd * XATTN_HEAD_DIM, (hd + 1) * XATTN_HEAD_DIM)
        s = _dot(qx[:, hs], kt_ref[0, hs, :])
        e = jnp.exp(s - jnp.max(s, axis=-1, keepdims=True))
        p = e * (1.0 / jnp.sum(e, axis=-1, keepdims=True))
        outs.append(_dot(p.astype(BF16), vm_ref[0, :, hs]).astype(BF16))
    x2 = x1 + _dot(jnp.concatenate(outs, axis=1), wxo_ref[...])

    hm = _rms(x2, mnw_ref[...]).astype(BF16)
    acc = jnp.zeros((t, D_MODEL), F32)
    for c in range(D_FF // FF_CHUNK):
        cs = slice(c * FF_CHUNK, (c + 1) * FF_CHUNK)
        hc = jnp.maximum(_dot(hm, w1_ref[:, cs]), 0.0)
        acc = acc + _dot((hc * hc).astype(BF16), w2_ref[cs, :])
    x3 = x2 + acc

    for piece in _conv_pieces(u1_ref, up1_ref, un1_ref, tile1 > 0, tile1 < n_tiles - 1, *conv_consts):
        piece()
    o_ref[0] = _rms(x3, fnw_ref[...])


def _block_rest(x, q, k, v, sg, u, rf, lgf_lanes, lgb_lanes, gn_w, gn_b, conv_w, conv_b, ln_w, ln_b,
                w_out, xattn_norm_w, kt, vm, w_xq, w_xo, mlp_norm_w, w_ff1, w_ff2, final_norm_w):
    b, s, d = x.shape
    t = REST_TILE
    R = RET_WIDTH
    n_tiles = s // t
    n_steps = b * n_tiles
    hb = t // HALO
    n_hb = s // HALO

    def batch_tile(step):
        return step // n_tiles, n_tiles - 1 - step % n_tiles

    def ahead(step):
        return jnp.minimum(step + 1, n_steps - 1)

    def main_map(step):
        bi, ti = batch_tile(step)
        return bi, ti, 0

    def prev_halo_map(step):
        bi, ti = batch_tile(step)
        return bi, jnp.maximum(ti * hb - 1, 0), 0

    def next_halo_map(step):
        bi, ti = batch_tile(step)
        return bi, jnp.minimum((ti + 1) * hb, n_hb - 1), 0

    act_spec = pl.BlockSpec((1, t, R), main_map)
    halo_shape = (1, HALO, R)
    m = kt.shape[2]
    return pl.pallas_call(
        functools.partial(_rest_body, n_tiles),
        grid=(n_steps,),
        in_specs=[pl.BlockSpec((1, t, d), main_map),
                  act_spec, act_spec, act_spec, act_spec, act_spec,
                  pl.BlockSpec(halo_shape, prev_halo_map),
                  pl.BlockSpec(halo_shape, next_halo_map),
                  pl.BlockSpec((1, t, R), lambda st: main_map(ahead(st))),
                  pl.BlockSpec(halo_shape, lambda st: prev_halo_map(ahead(st))),
                  pl.BlockSpec(halo_shape, lambda st: next_halo_map(ahead(st))),
                  pl.BlockSpec((1, t // CHUNK, CHUNK, R), lambda st: main_map(st) + (0,)),
                  _const_spec((1, R)), _const_spec((1, R)),
                  _const_spec((1, R)), _const_spec((1, R)),
                  _const_spec((CONV_KERNEL, CONV_WIDTH)),
                  _const_spec((1, CONV_WIDTH)), _const_spec((1, CONV_WIDTH)), _const_spec((1, CONV_WIDTH)),
                  _const_spec((d, d)),
                  _const_spec((1, d)),
                  pl.BlockSpec((1, d, m), lambda st: (st // n_tiles, 0, 0)),
                  pl.BlockSpec((1, m, d), lambda st: (st // n_tiles, 0, 0)),
                  _const_spec((d, d)), _const_spec((d, d)),
                  _const_spec((1, d)),
                  _const_spec((d, D_FF)), _const_spec((D_FF, d)),
                  _const_spec((1, d))],
        out_specs=pl.BlockSpec((1, t, d), main_map),
        out_shape=jax.ShapeDtypeStruct((b, s, d), F32),
        scratch_shapes=[pltpu.VMEM((CHUNK, R), F32),
                        pltpu.VMEM((t, D_MODEL), BF16),
                        pltpu.VMEM((t + 2 * HALO, CONV_WIDTH), F32),
                        pltpu.VMEM((SUBLANES, CONV_SPAN + SHIFT_ROWS, CONV_WIDTH), F32),
                        pltpu.VMEM((t, CONV_WIDTH), BF16)],
        compiler_params=pltpu.CompilerParams(
            dimension_semantics=("arbitrary",), vmem_limit_bytes=VMEM_LIMIT_BYTES),
        name="block_rest",
    )(x, q, k, v, sg, u, u, u, u, u, u, rf, lgf_lanes, lgb_lanes, gn_w, gn_b, conv_w, conv_b, ln_w, ln_b,
      w_out, xattn_norm_w, kt, vm, w_xq, w_xo, mlp_norm_w, w_ff1, w_ff2, final_norm_w)


def kernel(x, mem, positions, mix_norm_w, w_in, ret_decay_f, ret_decay_b, ret_gn_w, ret_gn_b,
           conv_w, conv_b, conv_ln_w, conv_ln_b, w_out, xattn_norm_w, mem_norm_w,
           w_xq, w_xkv, w_xo, mlp_norm_w, w_ff1, w_ff2, final_norm_w):
    assert w_in.shape[0] == 1, "single-layer block"
    b, s, d = x.shape
    half = HEAD_DIM // 2
    inv_freq = ROPE_THETA ** (-jnp.arange(half, dtype=F32) / half)
    invf2 = jnp.concatenate([inv_freq, inv_freq])[None]
    sign2 = jnp.concatenate([-jnp.ones((half,), F32), jnp.ones((half,), F32)])[None]
    pos3 = positions.reshape(b, s, 1)
    row = lambda a: a.reshape(1, -1)
    lgf_lanes = jnp.repeat(jax.nn.log_sigmoid(ret_decay_f[0].astype(F32)), HEAD_DIM)[None]
    lgb_lanes = jnp.repeat(jax.nn.log_sigmoid(ret_decay_b[0].astype(F32)), HEAD_DIM)[None]
    kt, vm = _mem_kv(mem, row(mem_norm_w[0]), w_xkv[0].astype(BF16))
    q, k, v, sg, u, rf = _in_proj(x, pos3, row(mix_norm_w[0]), invf2, sign2, lgf_lanes,
                                  w_in[0].astype(BF16))
    return _block_rest(x, q, k, v, sg, u, rf, lgf_lanes, lgb_lanes,
                       row(ret_gn_w[0]), row(ret_gn_b[0]),
                       conv_w[0].reshape(CONV_KERNEL, CONV_WIDTH), row(conv_b[0]),
                       row(conv_ln_w[0]), row(conv_ln_b[0]),
                       w_out[0].astype(BF16), row(xattn_norm_w[0]), kt, vm,
                       w_xq[0].astype(BF16), w_xo[0].astype(BF16),
                       row(mlp_norm_w[0]), w_ff1[0].astype(BF16), w_ff2[0].astype(BF16),
                       row(final_norm_w))
```

```python
import functools

import jax
import jax.numpy as jnp
import numpy as np
from jax import lax
from jax.experimental import pallas as pl
from jax.experimental.pallas import tpu as pltpu

F32 = jnp.float32
BF16 = jnp.bfloat16

D_MODEL = 1024
RET_WIDTH = 512
CONV_WIDTH = 512
RET_HEADS = 4
HEAD_DIM = 128
CHUNK = 128
CONV_KERNEL = 31
CONV_PAD = CONV_KERNEL // 2
HALO = 16
XATTN_HEADS = 4
XATTN_HEAD_DIM = 256
D_FF = 4096
ROPE_THETA = 10000.0
EPS = 1e-6

IN_PROJ_TILE = 1024
REST_TILE = 512
FF_CHUNK = 1024
CONV_ROWS = 32
CONV_SPAN = 256
CAST_ROWS = 128
HALF_PI_1 = 1.5703125
HALF_PI_2 = 4.837512969970703125e-4
HALF_PI_3 = 7.54978995489188216e-8
SUBLANES = 8
BF16_SUBLANES = 16
LANES = 128
SHIFT_ROWS = ((HALO - CONV_PAD + CONV_KERNEL - 1) // SUBLANES) * SUBLANES
VMEM_LIMIT_BYTES = 58 * 1024 * 1024


def _rms(x, w):
    ms = jnp.mean(x * x, axis=-1, keepdims=True)
    return x * lax.rsqrt(ms + EPS) * w


def _dot(a, b):
    return jnp.dot(a, b, preferred_element_type=F32)


def _dot_tn(a, b):
    return lax.dot_general(a, b, (((0,), (0,)), ((), ())), preferred_element_type=F32)


def _dot_nt(a, b):
    return lax.dot_general(a, b, (((1,), (1,)), ((), ())), preferred_element_type=F32)


def _const_spec(shape):
    nd = len(shape)
    return pl.BlockSpec(shape, lambda *_: (0,) * nd, pipeline_mode=pl.Buffered(1))


def _mem_kv_body(mem_ref, nw_ref, wkv_ref, kt_ref, v_ref):
    m = _rms(mem_ref[0], nw_ref[...]).astype(BF16)
    kv = _dot(m, wkv_ref[...].astype(BF16))
    k = kv[:, :D_MODEL] * (XATTN_HEAD_DIM ** -0.5)
    kt_ref[0] = k.T.astype(BF16)
    v_ref[0] = kv[:, D_MODEL:].astype(BF16)


def _mem_kv(mem, mem_norm_w, w_xkv):
    b, m, d = mem.shape
    return pl.pallas_call(
        _mem_kv_body,
        grid=(b,),
        in_specs=[pl.BlockSpec((1, m, d), lambda i: (i, 0, 0)),
                  _const_spec((1, d)),
                  _const_spec((d, 2 * d))],
        out_specs=[pl.BlockSpec((1, d, m), lambda i: (i, 0, 0)),
                   pl.BlockSpec((1, m, d), lambda i: (i, 0, 0))],
        out_shape=[jax.ShapeDtypeStruct((b, d, m), BF16),
                   jax.ShapeDtypeStruct((b, m, d), BF16)],
        compiler_params=pltpu.CompilerParams(
            dimension_semantics=("arbitrary",), vmem_limit_bytes=VMEM_LIMIT_BYTES),
        name="mem_kv",
    )(mem, mem_norm_w, w_xkv)


def _cos_or_sin(x, want_sin):
    k = jnp.floor(x * (2.0 / np.pi) + 0.5)
    r = ((x - k * HALF_PI_1) - k * HALF_PI_2) - k * HALF_PI_3
    r2 = r * r
    sin_r = r + r * r2 * (-1.6666654611e-1 + r2 * (8.3321608736e-3 + r2 * -1.9515295891e-4))
    cos_r = 1.0 - 0.5 * r2 + r2 * r2 * (4.166664568298827e-2
                                        + r2 * (-1.388731625493765e-3 + r2 * 2.443315711809948e-5))
    kq = k + jnp.where(want_sin, 3.0, 0.0)
    q = kq - 4.0 * jnp.floor(kq * 0.25)
    hi = jnp.floor(q * 0.5)
    lo = q - 2.0 * hi
    return (1.0 - 2.0 * hi) * (cos_r - lo * (cos_r + sin_r))


def _in_proj_body(x_ref, pos_ref, nw_ref, invf_ref, lgf_ref, w_ref,
                  wout_ref, wxq_ref, wxo_ref, w1_ref, w2_ref,
                  q_ref, k_ref, v_ref, sg_ref, u_ref, rf_ref,
                  wout_o, wxq_o, wxo_o, w1_o, w2_o, state_ref, wbf_ref):
    t = x_ref.shape[1]
    R = RET_WIDTH

    @pl.when((pl.program_id(0) == 0) & (pl.program_id(1) == 0))
    def _():
        for r0 in range(0, D_MODEL, CAST_ROWS):
            wbf_ref[r0:r0 + CAST_ROWS, :] = w_ref[r0:r0 + CAST_ROWS, :].astype(BF16)

    @pl.when(pl.program_id(1) == 0)
    def _():
        state_ref[...] = jnp.zeros_like(state_ref)

    for src, dst in ((wout_ref, wout_o), (wxq_ref, wxq_o), (wxo_ref, wxo_o), (w1_ref, w1_o), (w2_ref, w2_o)):
        dst[...] = src[...].astype(BF16)

    h = _rms(x_ref[0], nw_ref[...]).astype(BF16)

    ang = pos_ref[0].astype(F32) * invf_ref[...]
    upper = lax.broadcasted_iota(jnp.int32, (1, HEAD_DIM), 1) >= HEAD_DIM // 2
    cs = _cos_or_sin(ang, upper)
    sc = pltpu.roll(cs, HEAD_DIM // 2, 1)
    cos2 = jnp.where(upper, sc, cs)
    sin2 = jnp.where(upper, cs, -sc)

    def rope(p):
        outs = []
        for hd in range(RET_HEADS):
            ph = p[:, hd * HEAD_DIM:(hd + 1) * HEAD_DIM]
            outs.append(ph * cos2 + pltpu.roll(ph, HEAD_DIM // 2, 1) * sin2)
        return jnp.concatenate(outs, axis=1)

    q = rope(_dot(h, wbf_ref[:, 0:R]))
    q_ref[0] = q.astype(BF16)
    k = rope(_dot(h, wbf_ref[:, R:2 * R])) * (HEAD_DIM ** -0.5)
    k_ref[0] = k.astype(BF16)
    v = _dot(h, wbf_ref[:, 2 * R:3 * R]).astype(BF16)
    v_ref[0] = v
    g = _dot(h, wbf_ref[:, 3 * R:4 * R])
    sg_ref[0] = (g * jax.nn.sigmoid(g)).astype(BF16)
    a = _dot(h, wbf_ref[:, 4 * R:5 * R])
    bb = _dot(h, wbf_ref[:, 5 * R:6 * R])
    u_ref[0] = (a * jax.nn.sigmoid(bb)).astype(BF16)

    lgf = lgf_ref[...]
    row = lax.broadcasted_iota(jnp.int32, (CHUNK, R), 0).astype(F32)
    zeta = jnp.exp(lgf * (CHUNK - 1.0 - row))
    chunk_decay = jnp.exp(lgf * float(CHUNK))
    state = state_ref[...]
    for c in range(t // CHUNK):
        rf_ref[0, c] = state.astype(BF16)
        rows = slice(c * CHUNK, (c + 1) * CHUNK)
        kz = (k[rows] * zeta).astype(BF16)
        vc = v[rows]
        upd = jnp.concatenate(
            [_dot_tn(kz[:, hd * HEAD_DIM:(hd + 1) * HEAD_DIM],
                     vc[:, hd * HEAD_DIM:(hd + 1) * HEAD_DIM]) for hd in range(RET_HEADS)], axis=1)
        state = chunk_decay * state + upd
    state_ref[...] = state


def _in_proj(x, pos3, mix_norm_w, invf2, lgf_lanes, w_in, later_weights):
    b, s, d = x.shape
    t = IN_PROJ_TILE
    R = RET_WIDTH
    n_j = s // t
    n_steps = b * n_j
    tok = lambda i, j: (i, j, 0)
    act_spec = pl.BlockSpec((1, t, R), tok)
    act_shape = jax.ShapeDtypeStruct((b, s, R), BF16)
    cast_specs = []
    for w in later_weights:
        rows, cols = w.shape
        assert rows % (n_steps * BF16_SUBLANES) == 0
        cast_specs.append(pl.BlockSpec((rows // n_steps, cols), lambda i, j: (i * n_j + j, 0)))
    return pl.pallas_call(
        _in_proj_body,
        grid=(b, n_j),
        in_specs=[pl.BlockSpec((1, t, d), tok),
                  pl.BlockSpec((1, t, 1), tok),
                  _const_spec((1, d)),
                  _const_spec((1, HEAD_DIM)),
                  _const_spec((1, R)),
                  _const_spec((d, 6 * R))] + cast_specs,
        out_specs=([act_spec] * 5 + [pl.BlockSpec((1, t // CHUNK, CHUNK, R), lambda i, j: (i, j, 0, 0))]
                   + cast_specs),
        out_shape=([act_shape] * 5 + [jax.ShapeDtypeStruct((b, s // CHUNK, CHUNK, R), BF16)]
                   + [jax.ShapeDtypeStruct(w.shape, BF16) for w in later_weights]),
        scratch_shapes=[pltpu.VMEM((CHUNK, R), F32),
                        pltpu.VMEM((d, 6 * R), BF16)],
        compiler_params=pltpu.CompilerParams(
            dimension_semantics=("arbitrary", "arbitrary"), vmem_limit_bytes=VMEM_LIMIT_BYTES),
        name="in_proj",
    )(x, pos3, mix_norm_w, invf2, lgf_lanes, w_in, *later_weights)


def _conv_pieces(u_ref, up_ref, un_ref, has_prev, has_next,
                 cw_ref, cb_ref, lnw_ref, lnb_ref, uext_ref, ush_ref, yconv_ref):
    t = u_ref.shape[1]
    span = CONV_SPAN + SHIFT_ROWS

    def fill():
        zero_halo = jnp.zeros((HALO, CONV_WIDTH), F32)
        uext_ref[0:HALO, :] = jnp.where(has_prev, up_ref[0].astype(F32), zero_halo)
        uext_ref[HALO:HALO + t, :] = u_ref[0].astype(F32)
        uext_ref[HALO + t:HALO + t + HALO, :] = jnp.where(has_next, un_ref[0].astype(F32), zero_halo)

    def shift(hb):
        for sft in range(1, SUBLANES):
            ush_ref[sft, 0:span, :] = uext_ref[hb + sft:hb + sft + span, :]

    def block(hb, base):
        acc = jnp.broadcast_to(cb_ref[...], (CONV_ROWS, CONV_WIDTH))
        for tap in range(CONV_KERNEL):
            grp, sft = divmod(HALO - CONV_PAD + tap, SUBLANES)
            lo = base + grp * SUBLANES
            if sft == 0:
                src = uext_ref[hb + lo:hb + lo + CONV_ROWS, :]
            else:
                src = ush_ref[sft, lo:lo + CONV_ROWS, :]
            acc = acc + cw_ref[tap:tap + 1, :] * src
        mu = jnp.mean(acc, axis=-1, keepdims=True)
        yc = acc - mu
        var = jnp.mean(yc * yc, axis=-1, keepdims=True)
        yl = yc * lax.rsqrt(var + EPS) * lnw_ref[...] + lnb_ref[...]
        yconv_ref[hb + base:hb + base + CONV_ROWS, :] = (yl * jax.nn.sigmoid(yl)).astype(BF16)

    pieces = []
    for hb in range(0, t, CONV_SPAN):
        for base in range(0, CONV_SPAN, CONV_ROWS):
            def piece(hb=hb, base=base):
                if hb == 0 and base == 0:
                    fill()
                if base == 0:
                    shift(hb)
                block(hb, base)
            pieces.append(piece)
    return pieces


def _rest_body(n_tiles,
               x_ref, q_ref, k_ref, v_ref, sg_ref, u_ref, up_ref, un_ref, u1_ref, up1_ref, un1_ref, rf_ref,
               lgf_ref, lgb_ref, gnw_ref, gnb_ref, cw_ref, cb_ref, lnw_ref, lnb_ref,
               wout_ref, xnw_ref, kt_ref, vm_ref, wxq_ref, wxo_ref,
               mnw_ref, w1_ref, w2_ref, fnw_ref,
               o_ref, rb_ref, mix_ref, uext_ref, ush_ref, yconv_ref):
    t = x_ref.shape[1]
    R = RET_WIDTH
    step = pl.program_id(0)
    last_step = pl.num_programs(0) - 1
    tile = n_tiles - 1 - step % n_tiles
    step1 = jnp.minimum(step + 1, last_step)
    tile1 = n_tiles - 1 - step1 % n_tiles
    conv_consts = (cw_ref, cb_ref, lnw_ref, lnb_ref, uext_ref, ush_ref, yconv_ref)

    @pl.when(step % n_tiles == 0)
    def _():
        rb_ref[...] = jnp.zeros_like(rb_ref)

    @pl.when(step == 0)
    def _():
        for piece in _conv_pieces(u_ref, up_ref, un_ref, tile > 0, tile < n_tiles - 1, *conv_consts):
            piece()

    mix_ref[:, R:R + CONV_WIDTH] = yconv_ref[...]

    lgf = lgf_ref[...]
    lgb = lgb_ref[...]
    row = lax.broadcasted_iota(jnp.int32, (CHUNK, R), 0).astype(F32)
    xi_f = jnp.exp(lgf * (row + 1.0))
    xi_b = jnp.exp(lgb * (float(CHUNK) - row))
    zeta_b = jnp.exp(lgb * row)
    decay_b = jnp.exp(lgb * float(CHUNK))
    ii = lax.broadcasted_iota(jnp.int32, (CHUNK, CHUNK), 0)
    jj = lax.broadcasted_iota(jnp.int32, (CHUNK, CHUNK), 1)
    diff = (ii - jj).astype(F32)
    dmask = []
    for hd in range(RET_HEADS):
        hs = slice(hd * HEAD_DIM, (hd + 1) * HEAD_DIM)
        dmask.append(jnp.where(diff >= 0.0,
                               jnp.exp(lgf[:, hs] * jnp.maximum(diff, 0.0)),
                               jnp.exp(lgb[:, hs] * jnp.maximum(-diff, 0.0))))

    gnw = gnw_ref[...]
    gnb = gnb_ref[...]
    rb = rb_ref[...]
    for c in reversed(range(t // CHUNK)):
        rows = slice(c * CHUNK, (c + 1) * CHUNK)
        qc = q_ref[0, rows, :]
        kc = k_ref[0, rows, :]
        vc = v_ref[0, rows, :]
        sgc = sg_ref[0, rows, :].astype(F32)
        rfc = rf_ref[0, c]
        rbb = rb.astype(BF16)
        kzb = (kc.astype(F32) * zeta_b).astype(BF16)
        ys = []
        upds = []
        for hd in range(RET_HEADS):
            hs = slice(hd * HEAD_DIM, (hd + 1) * HEAD_DIM)
            qh, kh, vh = qc[:, hs], kc[:, hs], vc[:, hs]
            p = (_dot_nt(qh, kh) * dmask[hd]).astype(BF16)
            inner = _dot(p, vh)
            cr = _dot(qh, jnp.concatenate([rfc[:, hs], rbb[:, hs]], axis=1))
            y = inner + cr[:, :HEAD_DIM] * xi_f[:, hs] + cr[:, HEAD_DIM:] * xi_b[:, hs]
            mu = jnp.mean(y, axis=-1, keepdims=True)
            yc = y - mu
            var = jnp.mean(yc * yc, axis=-1, keepdims=True)
            ys.append(yc * lax.rsqrt(var + EPS))
            upds.append(_dot_tn(kzb[:, hs], vh))
        yn = jnp.concatenate(ys, axis=1) * gnw + gnb
        mix_ref[rows, 0:R] = (sgc * yn).astype(BF16)
        rb = decay_b * rb + jnp.concatenate(upds, axis=1)
    rb_ref[...] = rb

    x1 = x_ref[0] + _dot(mix_ref[...], wout_ref[...])

    h2 = _rms(x1, xnw_ref[...]).astype(BF16)
    qx = _dot(h2, wxq_ref[...]).astype(BF16)
    outs = []
    for hd in range(XATTN_HEADS):
        hs = slice(hd * XATTN_HEAD_DIM, (hd + 1) * XATTN_HEAD_DIM)
        s = _dot(qx[:, hs], kt_ref[0, hs, :])
        e = jnp.exp(s - jnp.max(s, axis=-1, keepdims=True))
        p = e * (1.0 / jnp.sum(e, axis=-1, keepdims=True))
        outs.append(_dot(p.astype(BF16), vm_ref[0, :, hs]).astype(BF16))
    x2 = x1 + _dot(jnp.concatenate(outs, axis=1), wxo_ref[...])

    hm = _rms(x2, mnw_ref[...]).astype(BF16)
    acc = jnp.zeros((t, D_MODEL), F32)
    for c in range(D_FF // FF_CHUNK):
        cs = slice(c * FF_CHUNK, (c + 1) * FF_CHUNK)
        hc = jnp.maximum(_dot(hm, w1_ref[:, cs]), 0.0)
        acc = acc + _dot((hc * hc).astype(BF16), w2_ref[cs, :])
    x3 = x2 + acc

    for piece in _conv_pieces(u1_ref, up1_ref, un1_ref, tile1 > 0, tile1 < n_tiles - 1, *conv_consts):
        piece()
    o_ref[0] = _rms(x3, fnw_ref[...])


def _block_rest(x, q, k, v, sg, u, rf, lgf_lanes, lgb_lanes, gn_w, gn_b, conv_w, conv_b, ln_w, ln_b,
                w_out, xattn_norm_w, kt, vm, w_xq, w_xo, mlp_norm_w, w_ff1, w_ff2, final_norm_w):
    b, s, d = x.shape
    t = REST_TILE
    R = RET_WIDTH
    n_tiles = s // t
    n_steps = b * n_tiles
    hb = t // HALO
    n_hb = s // HALO

    def batch_tile(step):
        return step // n_tiles, n_tiles - 1 - step % n_tiles

    def ahead(step):
        return jnp.minimum(step + 1, n_steps - 1)

    def main_map(step):
        bi, ti = batch_tile(step)
        return bi, ti, 0

    def prev_halo_map(step):
        bi, ti = batch_tile(step)
        return bi, jnp.maximum(ti * hb - 1, 0), 0

    def next_halo_map(step):
        bi, ti = batch_tile(step)
        return bi, jnp.minimum((ti + 1) * hb, n_hb - 1), 0

    act_spec = pl.BlockSpec((1, t, R), main_map)
    halo_shape = (1, HALO, R)
    m = kt.shape[2]
    return pl.pallas_call(
        functools.partial(_rest_body, n_tiles),
        grid=(n_steps,),
        in_specs=[pl.BlockSpec((1, t, d), main_map),
                  act_spec, act_spec, act_spec, act_spec, act_spec,
                  pl.BlockSpec(halo_shape, prev_halo_map),
                  pl.BlockSpec(halo_shape, next_halo_map),
                  pl.BlockSpec((1, t, R), lambda st: main_map(ahead(st))),
                  pl.BlockSpec(halo_shape, lambda st: prev_halo_map(ahead(st))),
                  pl.BlockSpec(halo_shape, lambda st: next_halo_map(ahead(st))),
                  pl.BlockSpec((1, t // CHUNK, CHUNK, R), lambda st: main_map(st) + (0,)),
                  _const_spec((1, R)), _const_spec((1, R)),
                  _const_spec((1, R)), _const_spec((1, R)),
                  _const_spec((CONV_KERNEL, CONV_WIDTH)),
                  _const_spec((1, CONV_WIDTH)), _const_spec((1, CONV_WIDTH)), _const_spec((1, CONV_WIDTH)),
                  _const_spec((d, d)),
                  _const_spec((1, d)),
                  pl.BlockSpec((1, d, m), lambda st: (st // n_tiles, 0, 0)),
                  pl.BlockSpec((1, m, d), lambda st: (st // n_tiles, 0, 0)),
                  _const_spec((d, d)), _const_spec((d, d)),
                  _const_spec((1, d)),
                  _const_spec((d, D_FF)), _const_spec((D_FF, d)),
                  _const_spec((1, d))],
        out_specs=pl.BlockSpec((1, t, d), main_map),
        out_shape=jax.ShapeDtypeStruct((b, s, d), F32),
        scratch_shapes=[pltpu.VMEM((CHUNK, R), F32),
                        pltpu.VMEM((t, D_MODEL), BF16),
                        pltpu.VMEM((t + 2 * HALO, CONV_WIDTH), F32),
                        pltpu.VMEM((SUBLANES, CONV_SPAN + SHIFT_ROWS, CONV_WIDTH), F32),
                        pltpu.VMEM((t, CONV_WIDTH), BF16)],
        compiler_params=pltpu.CompilerParams(
            dimension_semantics=("arbitrary",), vmem_limit_bytes=VMEM_LIMIT_BYTES),
        name="block_rest",
    )(x, q, k, v, sg, u, u, u, u, u, u, rf, lgf_lanes, lgb_lanes, gn_w, gn_b, conv_w, conv_b, ln_w, ln_b,
      w_out, xattn_norm_w, kt, vm, w_xq, w_xo, mlp_norm_w, w_ff1, w_ff2, final_norm_w)


def kernel(x, mem, positions, mix_norm_w, w_in, ret_decay_f, ret_decay_b, ret_gn_w, ret_gn_b,
           conv_w, conv_b, conv_ln_w, conv_ln_b, w_out, xattn_norm_w, mem_norm_w,
           w_xq, w_xkv, w_xo, mlp_norm_w, w_ff1, w_ff2, final_norm_w):
    assert w_in.shape[0] == 1, "single-layer block"
    b, s, d = x.shape
    half = HEAD_DIM // 2
    inv_freq = ROPE_THETA ** (-jnp.arange(half, dtype=F32) / half)
    invf2 = jnp.concatenate([inv_freq, inv_freq])[None]
    pos3 = positions.reshape(b, s, 1)
    row = lambda a: a.reshape(1, -1)
    lgf_lanes = jnp.repeat(jax.nn.log_sigmoid(ret_decay_f[0].astype(F32)), HEAD_DIM)[None]
    lgb_lanes = jnp.repeat(jax.nn.log_sigmoid(ret_decay_b[0].astype(F32)), HEAD_DIM)[None]
    kt, vm = _mem_kv(mem, row(mem_norm_w[0]), w_xkv[0])
    (q, k, v, sg, u, rf, w_out_b, w_xq_b, w_xo_b, w_ff1_b, w_ff2_b) = _in_proj(
        x, pos3, row(mix_norm_w[0]), invf2, lgf_lanes, w_in[0],
        (w_out[0], w_xq[0], w_xo[0], w_ff1[0], w_ff2[0]))
    return _block_rest(x, q, k, v, sg, u, rf, lgf_lanes, lgb_lanes,
                       row(ret_gn_w[0]), row(ret_gn_b[0]),
                       conv_w[0].reshape(CONV_KERNEL, CONV_WIDTH), row(conv_b[0]),
                       row(conv_ln_w[0]), row(conv_ln_b[0]),
                       w_out_b, row(xattn_norm_w[0]), kt, vm, w_xq_b, w_xo_b,
                       row(mlp_norm_w[0]), w_ff1_b, w_ff2_b, row(final_norm_w))
```

```python
import functools

import jax
import jax.numpy as jnp
import numpy as np
from jax import lax
from jax.experimental import pallas as pl
from jax.experimental.pallas import tpu as pltpu

F32 = jnp.float32
BF16 = jnp.bfloat16

D_MODEL = 1024
RET_WIDTH = 512
CONV_WIDTH = 512
RET_HEADS = 4
HEAD_DIM = 128
CHUNK = 128
CONV_KERNEL = 31
CONV_PAD = CONV_KERNEL // 2
HALO = 16
XATTN_HEADS = 4
XATTN_HEAD_DIM = 256
D_FF = 4096
ROPE_THETA = 10000.0
EPS = 1e-6

IN_PROJ_TILE = 1024
REST_TILE = 512
FF_CHUNK = 1024
CONV_ROWS = 32
CONV_SPAN = 256
CAST_ROWS = 128
HALF_PI_1 = 1.5703125
HALF_PI_2 = 4.837512969970703125e-4
HALF_PI_3 = 7.54978995489188216e-8
SUBLANES = 8
BF16_SUBLANES = 16
LANES = 128
SHIFT_ROWS = ((HALO - CONV_PAD + CONV_KERNEL - 1) // SUBLANES) * SUBLANES
VMEM_LIMIT_BYTES = 58 * 1024 * 1024


def _rms(x, w):
    ms = jnp.mean(x * x, axis=-1, keepdims=True)
    return x * lax.rsqrt(ms + EPS) * w


def _dot(a, b):
    return jnp.dot(a, b, preferred_element_type=F32)


def _dot_tn(a, b):
    return lax.dot_general(a, b, (((0,), (0,)), ((), ())), preferred_element_type=F32)


def _dot_nt(a, b):
    return lax.dot_general(a, b, (((1,), (1,)), ((), ())), preferred_element_type=F32)


def _block_diag(a):
    n, w2 = a.shape
    zero = jnp.zeros((n, w2 // 2), a.dtype)
    return jnp.concatenate([jnp.concatenate([a[:, :w2 // 2], zero], axis=1),
                            jnp.concatenate([zero, a[:, w2 // 2:]], axis=1)], axis=0)


def _const_spec(shape):
    nd = len(shape)
    return pl.BlockSpec(shape, lambda *_: (0,) * nd, pipeline_mode=pl.Buffered(1))


def _mem_kv_body(mem_ref, nw_ref, wkv_ref, kt_ref, v_ref):
    m = _rms(mem_ref[0], nw_ref[...]).astype(BF16)
    kv = _dot(m, wkv_ref[...].astype(BF16))
    k = kv[:, :D_MODEL] * (XATTN_HEAD_DIM ** -0.5)
    kt_ref[0] = k.T.astype(BF16)
    v_ref[0] = kv[:, D_MODEL:].astype(BF16)


def _mem_kv(mem, mem_norm_w, w_xkv):
    b, m, d = mem.shape
    return pl.pallas_call(
        _mem_kv_body,
        grid=(b,),
        in_specs=[pl.BlockSpec((1, m, d), lambda i: (i, 0, 0)),
                  _const_spec((1, d)),
                  _const_spec((d, 2 * d))],
        out_specs=[pl.BlockSpec((1, d, m), lambda i: (i, 0, 0)),
                   pl.BlockSpec((1, m, d), lambda i: (i, 0, 0))],
        out_shape=[jax.ShapeDtypeStruct((b, d, m), BF16),
                   jax.ShapeDtypeStruct((b, m, d), BF16)],
        compiler_params=pltpu.CompilerParams(
            dimension_semantics=("arbitrary",), vmem_limit_bytes=VMEM_LIMIT_BYTES),
        name="mem_kv",
    )(mem, mem_norm_w, w_xkv)


def _cos_or_sin(x, want_sin):
    k = jnp.floor(x * (2.0 / np.pi) + 0.5)
    r = ((x - k * HALF_PI_1) - k * HALF_PI_2) - k * HALF_PI_3
    r2 = r * r
    sin_r = r + r * r2 * (-1.6666654611e-1 + r2 * (8.3321608736e-3 + r2 * -1.9515295891e-4))
    cos_r = 1.0 - 0.5 * r2 + r2 * r2 * (4.166664568298827e-2
                                        + r2 * (-1.388731625493765e-3 + r2 * 2.443315711809948e-5))
    kq = k + jnp.where(want_sin, 3.0, 0.0)
    q = kq - 4.0 * jnp.floor(kq * 0.25)
    hi = jnp.floor(q * 0.5)
    lo = q - 2.0 * hi
    return (1.0 - 2.0 * hi) * (cos_r - lo * (cos_r + sin_r))


def _in_proj_body(x_ref, pos_ref, nw_ref, invf_ref, lgf_ref, w_ref,
                  wout_ref, wxq_ref, wxo_ref, w1_ref, w2_ref,
                  q_ref, k_ref, v_ref, sg_ref, u_ref, rf_ref,
                  wout_o, wxq_o, wxo_o, w1_o, w2_o, state_ref, wbf_ref):
    t = x_ref.shape[1]
    R = RET_WIDTH

    @pl.when((pl.program_id(0) == 0) & (pl.program_id(1) == 0))
    def _():
        for r0 in range(0, D_MODEL, CAST_ROWS):
            wbf_ref[r0:r0 + CAST_ROWS, :] = w_ref[r0:r0 + CAST_ROWS, :].astype(BF16)

    @pl.when(pl.program_id(1) == 0)
    def _():
        state_ref[...] = jnp.zeros_like(state_ref)

    for src, dst in ((wout_ref, wout_o), (wxq_ref, wxq_o), (wxo_ref, wxo_o), (w1_ref, w1_o), (w2_ref, w2_o)):
        dst[...] = src[...].astype(BF16)

    h = _rms(x_ref[0], nw_ref[...]).astype(BF16)

    ang = pos_ref[0].astype(F32) * invf_ref[...]
    upper = lax.broadcasted_iota(jnp.int32, (1, HEAD_DIM), 1) >= HEAD_DIM // 2
    cs = _cos_or_sin(ang, upper)
    sc = pltpu.roll(cs, HEAD_DIM // 2, 1)
    cos2 = jnp.where(upper, sc, cs)
    sin2 = jnp.where(upper, cs, -sc)

    def rope(p):
        outs = []
        for hd in range(RET_HEADS):
            ph = p[:, hd * HEAD_DIM:(hd + 1) * HEAD_DIM]
            outs.append(ph * cos2 + pltpu.roll(ph, HEAD_DIM // 2, 1) * sin2)
        return jnp.concatenate(outs, axis=1)

    q = rope(_dot(h, wbf_ref[:, 0:R]))
    q_ref[0] = q.astype(BF16)
    k = rope(_dot(h, wbf_ref[:, R:2 * R])) * (HEAD_DIM ** -0.5)
    k_ref[0] = k.astype(BF16)
    v = _dot(h, wbf_ref[:, 2 * R:3 * R]).astype(BF16)
    v_ref[0] = v
    g = _dot(h, wbf_ref[:, 3 * R:4 * R])
    sg_ref[0] = (g * jax.nn.sigmoid(g)).astype(BF16)
    a = _dot(h, wbf_ref[:, 4 * R:5 * R])
    bb = _dot(h, wbf_ref[:, 5 * R:6 * R])
    u_ref[0] = (a * jax.nn.sigmoid(bb)).astype(BF16)

    lgf = lgf_ref[...]
    row = lax.broadcasted_iota(jnp.int32, (CHUNK, R), 0).astype(F32)
    zeta = jnp.exp(lgf * (CHUNK - 1.0 - row))
    chunk_decay = jnp.exp(lgf * float(CHUNK))
    state = state_ref[...]
    for c in range(t // CHUNK):
        rf_ref[0, c] = state.astype(BF16)
        rows = slice(c * CHUNK, (c + 1) * CHUNK)
        kz = (k[rows] * zeta).astype(BF16)
        vc = v[rows]
        upd = jnp.concatenate(
            [_dot_tn(kz[:, hd * HEAD_DIM:(hd + 1) * HEAD_DIM],
                     vc[:, hd * HEAD_DIM:(hd + 1) * HEAD_DIM]) for hd in range(RET_HEADS)], axis=1)
        state = chunk_decay * state + upd
    state_ref[...] = state


def _in_proj(x, pos3, mix_norm_w, invf2, lgf_lanes, w_in, later_weights):
    b, s, d = x.shape
    t = IN_PROJ_TILE
    R = RET_WIDTH
    n_j = s // t
    n_steps = b * n_j
    tok = lambda i, j: (i, j, 0)
    act_spec = pl.BlockSpec((1, t, R), tok)
    act_shape = jax.ShapeDtypeStruct((b, s, R), BF16)
    cast_specs = []
    for w in later_weights:
        rows, cols = w.shape
        assert rows % (n_steps * BF16_SUBLANES) == 0
        cast_specs.append(pl.BlockSpec((rows // n_steps, cols), lambda i, j: (i * n_j + j, 0)))
    return pl.pallas_call(
        _in_proj_body,
        grid=(b, n_j),
        in_specs=[pl.BlockSpec((1, t, d), tok),
                  pl.BlockSpec((1, t, 1), tok),
                  _const_spec((1, d)),
                  _const_spec((1, HEAD_DIM)),
                  _const_spec((1, R)),
                  _const_spec((d, 6 * R))] + cast_specs,
        out_specs=([act_spec] * 5 + [pl.BlockSpec((1, t // CHUNK, CHUNK, R), lambda i, j: (i, j, 0, 0))]
                   + cast_specs),
        out_shape=([act_shape] * 5 + [jax.ShapeDtypeStruct((b, s // CHUNK, CHUNK, R), BF16)]
                   + [jax.ShapeDtypeStruct(w.shape, BF16) for w in later_weights]),
        scratch_shapes=[pltpu.VMEM((CHUNK, R), F32),
                        pltpu.VMEM((d, 6 * R), BF16)],
        compiler_params=pltpu.CompilerParams(
            dimension_semantics=("arbitrary", "arbitrary"), vmem_limit_bytes=VMEM_LIMIT_BYTES),
        name="in_proj",
    )(x, pos3, mix_norm_w, invf2, lgf_lanes, w_in, *later_weights)


def _conv_pieces(u_ref, up_ref, un_ref, has_prev, has_next,
                 cw_ref, cb_ref, lnw_ref, lnb_ref, uext_ref, ush_ref, yconv_ref):
    t = u_ref.shape[1]
    span = CONV_SPAN + SHIFT_ROWS

    def fill():
        zero_halo = jnp.zeros((HALO, CONV_WIDTH), F32)
        uext_ref[0:HALO, :] = jnp.where(has_prev, up_ref[0].astype(F32), zero_halo)
        uext_ref[HALO:HALO + t, :] = u_ref[0].astype(F32)
        uext_ref[HALO + t:HALO + t + HALO, :] = jnp.where(has_next, un_ref[0].astype(F32), zero_halo)

    def shift(hb):
        for sft in range(1, SUBLANES):
            ush_ref[sft, 0:span, :] = uext_ref[hb + sft:hb + sft + span, :]

    def block(hb, base):
        acc = jnp.broadcast_to(cb_ref[...], (CONV_ROWS, CONV_WIDTH))
        for tap in range(CONV_KERNEL):
            grp, sft = divmod(HALO - CONV_PAD + tap, SUBLANES)
            lo = base + grp * SUBLANES
            if sft == 0:
                src = uext_ref[hb + lo:hb + lo + CONV_ROWS, :]
            else:
                src = ush_ref[sft, lo:lo + CONV_ROWS, :]
            acc = acc + cw_ref[tap:tap + 1, :] * src
        mu = jnp.mean(acc, axis=-1, keepdims=True)
        yc = acc - mu
        var = jnp.mean(yc * yc, axis=-1, keepdims=True)
        yl = yc * lax.rsqrt(var + EPS) * lnw_ref[...] + lnb_ref[...]
        yconv_ref[hb + base:hb + base + CONV_ROWS, :] = (yl * jax.nn.sigmoid(yl)).astype(BF16)

    pieces = []
    for hb in range(0, t, CONV_SPAN):
        for base in range(0, CONV_SPAN, CONV_ROWS):
            def piece(hb=hb, base=base):
                if hb == 0 and base == 0:
                    fill()
                if base == 0:
                    shift(hb)
                block(hb, base)
            pieces.append(piece)
    return pieces


def _rest_body(n_tiles,
               x_ref, q_ref, k_ref, v_ref, sg_ref, u_ref, up_ref, un_ref, u1_ref, up1_ref, un1_ref, rf_ref,
               lgf_ref, lgb_ref, gnw_ref, gnb_ref, cw_ref, cb_ref, lnw_ref, lnb_ref,
               wout_ref, xnw_ref, kt_ref, vm_ref, wxq_ref, wxo_ref,
               mnw_ref, w1_ref, w2_ref, fnw_ref,
               o_ref, rb_ref, mix_ref, uext_ref, ush_ref, yconv_ref):
    t = x_ref.shape[1]
    R = RET_WIDTH
    step = pl.program_id(0)
    last_step = pl.num_programs(0) - 1
    tile = n_tiles - 1 - step % n_tiles
    step1 = jnp.minimum(step + 1, last_step)
    tile1 = n_tiles - 1 - step1 % n_tiles
    conv_consts = (cw_ref, cb_ref, lnw_ref, lnb_ref, uext_ref, ush_ref, yconv_ref)

    @pl.when(step % n_tiles == 0)
    def _():
        rb_ref[...] = jnp.zeros_like(rb_ref)

    @pl.when(step == 0)
    def _():
        for piece in _conv_pieces(u_ref, up_ref, un_ref, tile > 0, tile < n_tiles - 1, *conv_consts):
            piece()

    mix_ref[:, R:R + CONV_WIDTH] = yconv_ref[...]

    lgf = lgf_ref[...]
    lgb = lgb_ref[...]
    row = lax.broadcasted_iota(jnp.int32, (CHUNK, R), 0).astype(F32)
    xi_f = jnp.exp(lgf * (row + 1.0))
    xi_b = jnp.exp(lgb * (float(CHUNK) - row))
    zeta_b = jnp.exp(lgb * row)
    decay_b = jnp.exp(lgb * float(CHUNK))
    ii = lax.broadcasted_iota(jnp.int32, (CHUNK, CHUNK), 0)
    jj = lax.broadcasted_iota(jnp.int32, (CHUNK, CHUNK), 1)
    diff = (ii - jj).astype(F32)
    dmask = jnp.concatenate(
        [jnp.where(diff >= 0.0,
                   jnp.exp(lgf[:, hd * HEAD_DIM:(hd + 1) * HEAD_DIM] * jnp.maximum(diff, 0.0)),
                   jnp.exp(lgb[:, hd * HEAD_DIM:(hd + 1) * HEAD_DIM] * jnp.maximum(-diff, 0.0)))
         for hd in range(RET_HEADS)], axis=1)

    gnw = gnw_ref[...]
    gnb = gnb_ref[...]
    rb = rb_ref[...]
    for c in reversed(range(t // CHUNK)):
        rows = slice(c * CHUNK, (c + 1) * CHUNK)
        qc = q_ref[0, rows, :]
        kc = k_ref[0, rows, :]
        vc = v_ref[0, rows, :]
        sgc = sg_ref[0, rows, :].astype(F32)
        rfc = rf_ref[0, c]
        rbb = rb.astype(BF16)
        qf = qc.astype(F32)
        qxf = (qf * xi_f).astype(BF16)
        qxb = (qf * xi_b).astype(BF16)
        kzb = (kc.astype(F32) * zeta_b).astype(BF16)
        ys = []
        upds = []
        for ps in (slice(0, 2 * HEAD_DIM), slice(2 * HEAD_DIM, 4 * HEAD_DIM)):
            p = (_dot_nt(qc[:, ps], _block_diag(kc[:, ps])) * dmask[:, ps]).astype(BF16)
            lhs = jnp.concatenate([p, qxf[:, ps], qxb[:, ps]], axis=1)
            rhs = jnp.concatenate([_block_diag(vc[:, ps]), _block_diag(rfc[:, ps]), _block_diag(rbb[:, ps])],
                                  axis=0)
            ys.append(_dot(lhs, rhs))
            kv = _dot_tn(kzb[:, ps], vc[:, ps])
            upds += [kv[:HEAD_DIM, :HEAD_DIM], kv[HEAD_DIM:, HEAD_DIM:]]
        yn = []
        for hd, y in enumerate(ys):
            for half in (y[:, :HEAD_DIM], y[:, HEAD_DIM:]):
                mu = jnp.mean(half, axis=-1, keepdims=True)
                yc = half - mu
                var = jnp.mean(yc * yc, axis=-1, keepdims=True)
                yn.append(yc * lax.rsqrt(var + EPS))
        yn = jnp.concatenate(yn, axis=1) * gnw + gnb
        mix_ref[rows, 0:R] = (sgc * yn).astype(BF16)
        rb = decay_b * rb + jnp.concatenate(upds, axis=1)
    rb_ref[...] = rb

    x1 = x_ref[0] + _dot(mix_ref[...], wout_ref[...])

    h2 = _rms(x1, xnw_ref[...]).astype(BF16)
    qx = _dot(h2, wxq_ref[...]).astype(BF16)
    outs = []
    for hd in range(XATTN_HEADS):
        hs = slice(hd * XATTN_HEAD_DIM, (hd + 1) * XATTN_HEAD_DIM)
        s = _dot(qx[:, hs], kt_ref[0, hs, :])
        e = jnp.exp(s - jnp.max(s, axis=-1, keepdims=True))
        p = e * (1.0 / jnp.sum(e, axis=-1, keepdims=True))
        outs.append(_dot(p.astype(BF16), vm_ref[0, :, hs]).astype(BF16))
    x2 = x1 + _dot(jnp.concatenate(outs, axis=1), wxo_ref[...])

    hm = _rms(x2, mnw_ref[...]).astype(BF16)
    acc = jnp.zeros((t, D_MODEL), F32)
    for c in range(D_FF // FF_CHUNK):
        cs = slice(c * FF_CHUNK, (c + 1) * FF_CHUNK)
        hc = jnp.maximum(_dot(hm, w1_ref[:, cs]), 0.0)
        acc = acc + _dot((hc * hc).astype(BF16), w2_ref[cs, :])
    x3 = x2 + acc

    for piece in _conv_pieces(u1_ref, up1_ref, un1_ref, tile1 > 0, tile1 < n_tiles - 1, *conv_consts):
        piece()
    o_ref[0] = _rms(x3, fnw_ref[...])


def _block_rest(x, q, k, v, sg, u, rf, lgf_lanes, lgb_lanes, gn_w, gn_b, conv_w, conv_b, ln_w, ln_b,
                w_out, xattn_norm_w, kt, vm, w_xq, w_xo, mlp_norm_w, w_ff1, w_ff2, final_norm_w):
    b, s, d = x.shape
    t = REST_TILE
    R = RET_WIDTH
    n_tiles = s // t
    n_steps = b * n_tiles
    hb = t // HALO
    n_hb = s // HALO

    def batch_tile(step):
        return step // n_tiles, n_tiles - 1 - step % n_tiles

    def ahead(step):
        return jnp.minimum(step + 1, n_steps - 1)

    def main_map(step):
        bi, ti = batch_tile(step)
        return bi, ti, 0

    def prev_halo_map(step):
        bi, ti = batch_tile(step)
        return bi, jnp.maximum(ti * hb - 1, 0), 0

    def next_halo_map(step):
        bi, ti = batch_tile(step)
        return bi, jnp.minimum((ti + 1) * hb, n_hb - 1), 0

    act_spec = pl.BlockSpec((1, t, R), main_map)
    halo_shape = (1, HALO, R)
    m = kt.shape[2]
    return pl.pallas_call(
        functools.partial(_rest_body, n_tiles),
        grid=(n_steps,),
        in_specs=[pl.BlockSpec((1, t, d), main_map),
                  act_spec, act_spec, act_spec, act_spec, act_spec,
                  pl.BlockSpec(halo_shape, prev_halo_map),
                  pl.BlockSpec(halo_shape, next_halo_map),
                  pl.BlockSpec((1, t, R), lambda st: main_map(ahead(st))),
                  pl.BlockSpec(halo_shape, lambda st: prev_halo_map(ahead(st))),
                  pl.BlockSpec(halo_shape, lambda st: next_halo_map(ahead(st))),
                  pl.BlockSpec((1, t // CHUNK, CHUNK, R), lambda st: main_map(st) + (0,)),
                  _const_spec((1, R)), _const_spec((1, R)),
                  _const_spec((1, R)), _const_spec((1, R)),
                  _const_spec((CONV_KERNEL, CONV_WIDTH)),
                  _const_spec((1, CONV_WIDTH)), _const_spec((1, CONV_WIDTH)), _const_spec((1, CONV_WIDTH)),
                  _const_spec((d, d)),
                  _const_spec((1, d)),
                  pl.BlockSpec((1, d, m), lambda st: (st // n_tiles, 0, 0)),
                  pl.BlockSpec((1, m, d), lambda st: (st // n_tiles, 0, 0)),
                  _const_spec((d, d)), _const_spec((d, d)),
                  _const_spec((1, d)),
                  _const_spec((d, D_FF)), _const_spec((D_FF, d)),
                  _const_spec((1, d))],
        out_specs=pl.BlockSpec((1, t, d), main_map),
        out_shape=jax.ShapeDtypeStruct((b, s, d), F32),
        scratch_shapes=[pltpu.VMEM((CHUNK, R), F32),
                        pltpu.VMEM((t, D_MODEL), BF16),
                        pltpu.VMEM((t + 2 * HALO, CONV_WIDTH), F32),
                        pltpu.VMEM((SUBLANES, CONV_SPAN + SHIFT_ROWS, CONV_WIDTH), F32),
                        pltpu.VMEM((t, CONV_WIDTH), BF16)],
        compiler_params=pltpu.CompilerParams(
            dimension_semantics=("arbitrary",), vmem_limit_bytes=VMEM_LIMIT_BYTES),
        name="block_rest",
    )(x, q, k, v, sg, u, u, u, u, u, u, rf, lgf_lanes, lgb_lanes, gn_w, gn_b, conv_w, conv_b, ln_w, ln_b,
      w_out, xattn_norm_w, kt, vm, w_xq, w_xo, mlp_norm_w, w_ff1, w_ff2, final_norm_w)


def kernel(x, mem, positions, mix_norm_w, w_in, ret_decay_f, ret_decay_b, ret_gn_w, ret_gn_b,
           conv_w, conv_b, conv_ln_w, conv_ln_b, w_out, xattn_norm_w, mem_norm_w,
           w_xq, w_xkv, w_xo, mlp_norm_w, w_ff1, w_ff2, final_norm_w):
    assert w_in.shape[0] == 1, "single-layer block"
    b, s, d = x.shape
    half = HEAD_DIM // 2
    inv_freq = ROPE_THETA ** (-jnp.arange(half, dtype=F32) / half)
    invf2 = jnp.concatenate([inv_freq, inv_freq])[None]
    pos3 = positions.reshape(b, s, 1)
    row = lambda a: a.reshape(1, -1)
    lgf_lanes = jnp.repeat(jax.nn.log_sigmoid(ret_decay_f[0].astype(F32)), HEAD_DIM)[None]
    lgb_lanes = jnp.repeat(jax.nn.log_sigmoid(ret_decay_b[0].astype(F32)), HEAD_DIM)[None]
    kt, vm = _mem_kv(mem, row(mem_norm_w[0]), w_xkv[0])
    (q, k, v, sg, u, rf, w_out_b, w_xq_b, w_xo_b, w_ff1_b, w_ff2_b) = _in_proj(
        x, pos3, row(mix_norm_w[0]), invf2, lgf_lanes, w_in[0],
        (w_out[0], w_xq[0], w_xo[0], w_ff1[0], w_ff2[0]))
    return _block_rest(x, q, k, v, sg, u, rf, lgf_lanes, lgb_lanes,
                       row(ret_gn_w[0]), row(ret_gn_b[0]),
                       conv_w[0].reshape(CONV_KERNEL, CONV_WIDTH), row(conv_b[0]),
                       row(conv_ln_w[0]), row(conv_ln_b[0]),
                       w_out_b, row(xattn_norm_w[0]), kt, vm, w_xq_b, w_xo_b,
                       row(mlp_norm_w[0]), w_ff1_b, w_ff2_b, row(final_norm_w))
```

```python
import functools

import jax
import jax.numpy as jnp
import numpy as np
from jax import lax
from jax.experimental import pallas as pl
from jax.experimental.pallas import tpu as pltpu

F32 = jnp.float32
BF16 = jnp.bfloat16

D_MODEL = 1024
RET_WIDTH = 512
CONV_WIDTH = 512
RET_HEADS = 4
HEAD_DIM = 128
CHUNK = 128
CONV_KERNEL = 31
CONV_PAD = CONV_KERNEL // 2
HALO = 16
XATTN_HEADS = 4
XATTN_HEAD_DIM = 256
D_FF = 4096
ROPE_THETA = 10000.0
EPS = 1e-6

IN_PROJ_TILE = 1024
REST_TILE = 512
FF_CHUNK = 1024
CONV_ROWS = 32
CONV_SPAN = 256
CAST_ROWS = 128
HALF_PI_1 = 1.5703125
HALF_PI_2 = 4.837512969970703125e-4
HALF_PI_3 = 7.54978995489188216e-8
SUBLANES = 8
BF16_SUBLANES = 16
LANES = 128
SHIFT_ROWS = ((HALO - CONV_PAD + CONV_KERNEL - 1) // SUBLANES) * SUBLANES
VMEM_LIMIT_BYTES = 58 * 1024 * 1024


def _rms(x, w):
    ms = jnp.mean(x * x, axis=-1, keepdims=True)
    return x * lax.rsqrt(ms + EPS) * w


def _dot(a, b):
    return jnp.dot(a, b, preferred_element_type=F32)


def _dot_tn(a, b):
    return lax.dot_general(a, b, (((0,), (0,)), ((), ())), preferred_element_type=F32)


def _dot_nt(a, b):
    return lax.dot_general(a, b, (((1,), (1,)), ((), ())), preferred_element_type=F32)


def _block_diag(a):
    n, w2 = a.shape
    zero = jnp.zeros((n, w2 // 2), a.dtype)
    return jnp.concatenate([jnp.concatenate([a[:, :w2 // 2], zero], axis=1),
                            jnp.concatenate([zero, a[:, w2 // 2:]], axis=1)], axis=0)


def _const_spec(shape):
    nd = len(shape)
    return pl.BlockSpec(shape, lambda *_: (0,) * nd, pipeline_mode=pl.Buffered(1))


def _mem_kv_body(mem_ref, nw_ref, wkv_ref, kt_ref, v_ref):
    m = _rms(mem_ref[0], nw_ref[...]).astype(BF16)
    kv = _dot(m, wkv_ref[...].astype(BF16))
    k = kv[:, :D_MODEL] * (XATTN_HEAD_DIM ** -0.5)
    kt_ref[0] = k.T.astype(BF16)
    v_ref[0] = kv[:, D_MODEL:].astype(BF16)


def _mem_kv(mem, mem_norm_w, w_xkv):
    b, m, d = mem.shape
    return pl.pallas_call(
        _mem_kv_body,
        grid=(b,),
        in_specs=[pl.BlockSpec((1, m, d), lambda i: (i, 0, 0)),
                  _const_spec((1, d)),
                  _const_spec((d, 2 * d))],
        out_specs=[pl.BlockSpec((1, d, m), lambda i: (i, 0, 0)),
                   pl.BlockSpec((1, m, d), lambda i: (i, 0, 0))],
        out_shape=[jax.ShapeDtypeStruct((b, d, m), BF16),
                   jax.ShapeDtypeStruct((b, m, d), BF16)],
        compiler_params=pltpu.CompilerParams(
            dimension_semantics=("arbitrary",), vmem_limit_bytes=VMEM_LIMIT_BYTES),
        name="mem_kv",
    )(mem, mem_norm_w, w_xkv)


def _cos_or_sin(x, want_sin):
    k = jnp.floor(x * (2.0 / np.pi) + 0.5)
    r = ((x - k * HALF_PI_1) - k * HALF_PI_2) - k * HALF_PI_3
    r2 = r * r
    sin_r = r + r * r2 * (-1.6666654611e-1 + r2 * (8.3321608736e-3 + r2 * -1.9515295891e-4))
    cos_r = 1.0 - 0.5 * r2 + r2 * r2 * (4.166664568298827e-2
                                        + r2 * (-1.388731625493765e-3 + r2 * 2.443315711809948e-5))
    kq = k + jnp.where(want_sin, 3.0, 0.0)
    q = kq - 4.0 * jnp.floor(kq * 0.25)
    hi = jnp.floor(q * 0.5)
    lo = q - 2.0 * hi
    return (1.0 - 2.0 * hi) * (cos_r - lo * (cos_r + sin_r))


def _in_proj_body(x_ref, pos_ref, nw_ref, invf_ref, lgf_ref, w_ref,
                  wout_ref, wxq_ref, wxo_ref, w1_ref, w2_ref,
                  q_ref, k_ref, v_ref, sg_ref, u_ref, rf_ref,
                  wout_o, wxq_o, wxo_o, w1_o, w2_o, state_ref, wbf_ref):
    t = x_ref.shape[1]
    R = RET_WIDTH

    @pl.when((pl.program_id(0) == 0) & (pl.program_id(1) == 0))
    def _():
        for r0 in range(0, D_MODEL, CAST_ROWS):
            wbf_ref[r0:r0 + CAST_ROWS, :] = w_ref[r0:r0 + CAST_ROWS, :].astype(BF16)

    @pl.when(pl.program_id(1) == 0)
    def _():
        state_ref[...] = jnp.zeros_like(state_ref)

    for src, dst in ((wout_ref, wout_o), (wxq_ref, wxq_o), (wxo_ref, wxo_o), (w1_ref, w1_o), (w2_ref, w2_o)):
        dst[...] = src[...].astype(BF16)

    h = _rms(x_ref[0], nw_ref[...]).astype(BF16)

    ang_t = invf_ref[...] * pos_ref[0].astype(F32)
    cs = _cos_or_sin(ang_t, lax.broadcasted_iota(jnp.int32, (HEAD_DIM, 1), 0) >= HEAD_DIM // 2).T
    upper = lax.broadcasted_iota(jnp.int32, (1, HEAD_DIM), 1) >= HEAD_DIM // 2
    sc = pltpu.roll(cs, HEAD_DIM // 2, 1)
    cos2 = jnp.where(upper, sc, cs)
    sin2 = jnp.where(upper, cs, -sc)

    def rope(p):
        outs = []
        for hd in range(RET_HEADS):
            ph = p[:, hd * HEAD_DIM:(hd + 1) * HEAD_DIM]
            outs.append(ph * cos2 + pltpu.roll(ph, HEAD_DIM // 2, 1) * sin2)
        return jnp.concatenate(outs, axis=1)

    q = rope(_dot(h, wbf_ref[:, 0:R]))
    q_ref[0] = q.astype(BF16)
    k = rope(_dot(h, wbf_ref[:, R:2 * R])) * (HEAD_DIM ** -0.5)
    k_ref[0] = k.astype(BF16)
    v = _dot(h, wbf_ref[:, 2 * R:3 * R]).astype(BF16)
    v_ref[0] = v
    g = _dot(h, wbf_ref[:, 3 * R:4 * R])
    sg_ref[0] = (g * jax.nn.sigmoid(g)).astype(BF16)
    a = _dot(h, wbf_ref[:, 4 * R:5 * R])
    bb = _dot(h, wbf_ref[:, 5 * R:6 * R])
    u_ref[0] = (a * jax.nn.sigmoid(bb)).astype(BF16)

    lgf = lgf_ref[...]
    row = lax.broadcasted_iota(jnp.int32, (CHUNK, R), 0).astype(F32)
    zeta = jnp.exp(lgf * (CHUNK - 1.0 - row))
    chunk_decay = jnp.exp(lgf * float(CHUNK))
    state = state_ref[...]
    for c in range(t // CHUNK):
        rf_ref[0, c] = state.astype(BF16)
        rows = slice(c * CHUNK, (c + 1) * CHUNK)
        kz = (k[rows] * zeta).astype(BF16)
        vc = v[rows]
        upd = jnp.concatenate(
            [_dot_tn(kz[:, hd * HEAD_DIM:(hd + 1) * HEAD_DIM],
                     vc[:, hd * HEAD_DIM:(hd + 1) * HEAD_DIM]) for hd in range(RET_HEADS)], axis=1)
        state = chunk_decay * state + upd
    state_ref[...] = state


def _in_proj(x, pos3, mix_norm_w, invf2, lgf_lanes, w_in, later_weights):
    b, s, d = x.shape
    t = IN_PROJ_TILE
    R = RET_WIDTH
    n_j = s // t
    n_steps = b * n_j
    tok = lambda i, j: (i, j, 0)
    act_spec = pl.BlockSpec((1, t, R), tok)
    act_shape = jax.ShapeDtypeStruct((b, s, R), BF16)
    cast_specs = []
    for w in later_weights:
        rows, cols = w.shape
        assert rows % (n_steps * BF16_SUBLANES) == 0
        cast_specs.append(pl.BlockSpec((rows // n_steps, cols), lambda i, j: (i * n_j + j, 0)))
    return pl.pallas_call(
        _in_proj_body,
        grid=(b, n_j),
        in_specs=[pl.BlockSpec((1, t, d), tok),
                  pl.BlockSpec((1, 1, t), lambda i, j: (i, 0, j)),
                  _const_spec((1, d)),
                  _const_spec((HEAD_DIM, 1)),
                  _const_spec((1, R)),
                  _const_spec((d, 6 * R))] + cast_specs,
        out_specs=([act_spec] * 5 + [pl.BlockSpec((1, t // CHUNK, CHUNK, R), lambda i, j: (i, j, 0, 0))]
                   + cast_specs),
        out_shape=([act_shape] * 5 + [jax.ShapeDtypeStruct((b, s // CHUNK, CHUNK, R), BF16)]
                   + [jax.ShapeDtypeStruct(w.shape, BF16) for w in later_weights]),
        scratch_shapes=[pltpu.VMEM((CHUNK, R), F32),
                        pltpu.VMEM((d, 6 * R), BF16)],
        compiler_params=pltpu.CompilerParams(
            dimension_semantics=("arbitrary", "arbitrary"), vmem_limit_bytes=VMEM_LIMIT_BYTES),
        name="in_proj",
    )(x, pos3, mix_norm_w, invf2, lgf_lanes, w_in, *later_weights)


def _conv_pieces(u_ref, up_ref, un_ref, has_prev, has_next,
                 cpar_ref, uext_ref, ush_ref, yconv_ref):
    t = u_ref.shape[1]
    span = CONV_SPAN + SHIFT_ROWS

    def fill():
        zero_halo = jnp.zeros((HALO, CONV_WIDTH), F32)
        uext_ref[0:HALO, :] = jnp.where(has_prev, up_ref[0].astype(F32), zero_halo)
        uext_ref[HALO:HALO + t, :] = u_ref[0].astype(F32)
        uext_ref[HALO + t:HALO + t + HALO, :] = jnp.where(has_next, un_ref[0].astype(F32), zero_halo)

    def shift(hb):
        for sft in range(1, SUBLANES):
            ush_ref[sft, 0:span, :] = uext_ref[hb + sft:hb + sft + span, :]

    groups = (CONV_ROWS // SUBLANES, SUBLANES, CONV_WIDTH)

    def block(hb, base):
        acc = jnp.broadcast_to(cpar_ref[CONV_KERNEL][None], groups)
        for tap in range(CONV_KERNEL):
            grp, sft = divmod(HALO - CONV_PAD + tap, SUBLANES)
            lo = base + grp * SUBLANES
            if sft == 0:
                src = uext_ref[hb + lo:hb + lo + CONV_ROWS, :]
            else:
                src = ush_ref[sft, lo:lo + CONV_ROWS, :]
            acc = acc + cpar_ref[tap][None] * src.reshape(groups)
        mu = jnp.mean(acc, axis=-1, keepdims=True)
        yc = acc - mu
        var = jnp.mean(yc * yc, axis=-1, keepdims=True)
        yl = yc * lax.rsqrt(var + EPS) * cpar_ref[CONV_KERNEL + 1][None] + cpar_ref[CONV_KERNEL + 2][None]
        out = (yl * jax.nn.sigmoid(yl)).reshape(CONV_ROWS, CONV_WIDTH)
        yconv_ref[hb + base:hb + base + CONV_ROWS, :] = out.astype(BF16)

    pieces = []
    for hb in range(0, t, CONV_SPAN):
        for base in range(0, CONV_SPAN, CONV_ROWS):
            def piece(hb=hb, base=base):
                if hb == 0 and base == 0:
                    fill()
                if base == 0:
                    shift(hb)
                block(hb, base)
            pieces.append(piece)
    return pieces


def _rest_body(n_tiles,
               x_ref, q_ref, k_ref, v_ref, sg_ref, u_ref, up_ref, un_ref, u1_ref, up1_ref, un1_ref, rf_ref,
               lgf_ref, lgb_ref, gnw_ref, gnb_ref, cw_ref, cb_ref, lnw_ref, lnb_ref,
               wout_ref, xnw_ref, kt_ref, vm_ref, wxq_ref, wxo_ref,
               mnw_ref, w1_ref, w2_ref, fnw_ref,
               o_ref, rb_ref, mix_ref, uext_ref, ush_ref, yconv_ref, cpar_ref):
    t = x_ref.shape[1]
    R = RET_WIDTH
    step = pl.program_id(0)
    last_step = pl.num_programs(0) - 1
    tile = n_tiles - 1 - step % n_tiles
    step1 = jnp.minimum(step + 1, last_step)
    tile1 = n_tiles - 1 - step1 % n_tiles
    conv_consts = (cpar_ref, uext_ref, ush_ref, yconv_ref)

    @pl.when(step % n_tiles == 0)
    def _():
        rb_ref[...] = jnp.zeros_like(rb_ref)

    @pl.when(step == 0)
    def _():
        for tap in range(CONV_KERNEL):
            cpar_ref[tap] = jnp.broadcast_to(cw_ref[tap:tap + 1, :], (SUBLANES, CONV_WIDTH))
        for i, par in enumerate((cb_ref, lnw_ref, lnb_ref)):
            cpar_ref[CONV_KERNEL + i] = jnp.broadcast_to(par[...], (SUBLANES, CONV_WIDTH))
        for piece in _conv_pieces(u_ref, up_ref, un_ref, tile > 0, tile < n_tiles - 1, *conv_consts):
            piece()

    mix_ref[:, R:R + CONV_WIDTH] = yconv_ref[...]

    lgf = lgf_ref[...]
    lgb = lgb_ref[...]
    row = lax.broadcasted_iota(jnp.int32, (CHUNK, R), 0).astype(F32)
    xi_f = jnp.exp(lgf * (row + 1.0))
    xi_b = jnp.exp(lgb * (float(CHUNK) - row))
    zeta_b = jnp.exp(lgb * row)
    decay_b = jnp.exp(lgb * float(CHUNK))
    ii = lax.broadcasted_iota(jnp.int32, (CHUNK, CHUNK), 0)
    jj = lax.broadcasted_iota(jnp.int32, (CHUNK, CHUNK), 1)
    diff = (ii - jj).astype(F32)
    dmask = jnp.concatenate(
        [jnp.where(diff >= 0.0,
                   jnp.exp(lgf[:, hd * HEAD_DIM:(hd + 1) * HEAD_DIM] * jnp.maximum(diff, 0.0)),
                   jnp.exp(lgb[:, hd * HEAD_DIM:(hd + 1) * HEAD_DIM] * jnp.maximum(-diff, 0.0)))
         for hd in range(RET_HEADS)], axis=1)

    gnw = gnw_ref[...]
    gnb = gnb_ref[...]
    rb = rb_ref[...]
    for c in reversed(range(t // CHUNK)):
        rows = slice(c * CHUNK, (c + 1) * CHUNK)
        qc = q_ref[0, rows, :]
        kc = k_ref[0, rows, :]
        vc = v_ref[0, rows, :]
        sgc = sg_ref[0, rows, :].astype(F32)
        rfc = rf_ref[0, c]
        rbb = rb.astype(BF16)
        qf = qc.astype(F32)
        qxf = (qf * xi_f).astype(BF16)
        qxb = (qf * xi_b).astype(BF16)
        kzb = (kc.astype(F32) * zeta_b).astype(BF16)
        ys = []
        upds = []
        for ps in (slice(0, 2 * HEAD_DIM), slice(2 * HEAD_DIM, 4 * HEAD_DIM)):
            p = (_dot_nt(qc[:, ps], _block_diag(kc[:, ps])) * dmask[:, ps]).astype(BF16)
            lhs = jnp.concatenate([p, qxf[:, ps], qxb[:, ps]], axis=1)
            rhs = jnp.concatenate([_block_diag(vc[:, ps]), _block_diag(rfc[:, ps]), _block_diag(rbb[:, ps])],
                                  axis=0)
            ys.append(_dot(lhs, rhs))
            kv = _dot_tn(kzb[:, ps], vc[:, ps])
            upds += [kv[:HEAD_DIM, :HEAD_DIM], kv[HEAD_DIM:, HEAD_DIM:]]
        yn = []
        for hd, y in enumerate(ys):
            for half in (y[:, :HEAD_DIM], y[:, HEAD_DIM:]):
                mu = jnp.mean(half, axis=-1, keepdims=True)
                yc = half - mu
                var = jnp.mean(yc * yc, axis=-1, keepdims=True)
                yn.append(yc * lax.rsqrt(var + EPS))
        yn = jnp.concatenate(yn, axis=1) * gnw + gnb
        mix_ref[rows, 0:R] = (sgc * yn).astype(BF16)
        rb = decay_b * rb + jnp.concatenate(upds, axis=1)
    rb_ref[...] = rb

    x1 = x_ref[0] + _dot(mix_ref[...], wout_ref[...])

    h2 = _rms(x1, xnw_ref[...]).astype(BF16)
    qx = _dot(h2, wxq_ref[...]).astype(BF16)
    outs = []
    for hd in range(XATTN_HEADS):
        hs = slice(hd * XATTN_HEAD_DIM, (hd + 1) * XATTN_HEAD_DIM)
        s = _dot(qx[:, hs], kt_ref[0, hs, :])
        e = jnp.exp(s - jnp.max(s, axis=-1, keepdims=True))
        p = e * (1.0 / jnp.sum(e, axis=-1, keepdims=True))
        outs.append(_dot(p.astype(BF16), vm_ref[0, :, hs]).astype(BF16))
    x2 = x1 + _dot(jnp.concatenate(outs, axis=1), wxo_ref[...])

    hm = _rms(x2, mnw_ref[...]).astype(BF16)
    acc = jnp.zeros((t, D_MODEL), F32)
    for c in range(D_FF // FF_CHUNK):
        cs = slice(c * FF_CHUNK, (c + 1) * FF_CHUNK)
        hc = jnp.maximum(_dot(hm, w1_ref[:, cs]), 0.0)
        acc = acc + _dot((hc * hc).astype(BF16), w2_ref[cs, :])
    x3 = x2 + acc

    for piece in _conv_pieces(u1_ref, up1_ref, un1_ref, tile1 > 0, tile1 < n_tiles - 1, *conv_consts):
        piece()
    o_ref[0] = _rms(x3, fnw_ref[...])


def _block_rest(x, q, k, v, sg, u, rf, lgf_lanes, lgb_lanes, gn_w, gn_b, conv_w, conv_b, ln_w, ln_b,
                w_out, xattn_norm_w, kt, vm, w_xq, w_xo, mlp_norm_w, w_ff1, w_ff2, final_norm_w):
    b, s, d = x.shape
    t = REST_TILE
    R = RET_WIDTH
    n_tiles = s // t
    n_steps = b * n_tiles
    hb = t // HALO
    n_hb = s // HALO

    def batch_tile(step):
        return step // n_tiles, n_tiles - 1 - step % n_tiles

    def ahead(step):
        return jnp.minimum(step + 1, n_steps - 1)

    def main_map(step):
        bi, ti = batch_tile(step)
        return bi, ti, 0

    def prev_halo_map(step):
        bi, ti = batch_tile(step)
        return bi, jnp.maximum(ti * hb - 1, 0), 0

    def next_halo_map(step):
        bi, ti = batch_tile(step)
        return bi, jnp.minimum((ti + 1) * hb, n_hb - 1), 0

    act_spec = pl.BlockSpec((1, t, R), main_map)
    halo_shape = (1, HALO, R)
    m = kt.shape[2]
    return pl.pallas_call(
        functools.partial(_rest_body, n_tiles),
        grid=(n_steps,),
        in_specs=[pl.BlockSpec((1, t, d), main_map),
                  act_spec, act_spec, act_spec, act_spec, act_spec,
                  pl.BlockSpec(halo_shape, prev_halo_map),
                  pl.BlockSpec(halo_shape, next_halo_map),
                  pl.BlockSpec((1, t, R), lambda st: main_map(ahead(st))),
                  pl.BlockSpec(halo_shape, lambda st: prev_halo_map(ahead(st))),
                  pl.BlockSpec(halo_shape, lambda st: next_halo_map(ahead(st))),
                  pl.BlockSpec((1, t // CHUNK, CHUNK, R), lambda st: main_map(st) + (0,)),
                  _const_spec((1, R)), _const_spec((1, R)),
                  _const_spec((1, R)), _const_spec((1, R)),
                  _const_spec((CONV_KERNEL, CONV_WIDTH)),
                  _const_spec((1, CONV_WIDTH)), _const_spec((1, CONV_WIDTH)), _const_spec((1, CONV_WIDTH)),
                  _const_spec((d, d)),
                  _const_spec((1, d)),
                  pl.BlockSpec((1, d, m), lambda st: (st // n_tiles, 0, 0)),
                  pl.BlockSpec((1, m, d), lambda st: (st // n_tiles, 0, 0)),
                  _const_spec((d, d)), _const_spec((d, d)),
                  _const_spec((1, d)),
                  _const_spec((d, D_FF)), _const_spec((D_FF, d)),
                  _const_spec((1, d))],
        out_specs=pl.BlockSpec((1, t, d), main_map),
        out_shape=jax.ShapeDtypeStruct((b, s, d), F32),
        scratch_shapes=[pltpu.VMEM((CHUNK, R), F32),
                        pltpu.VMEM((t, D_MODEL), BF16),
                        pltpu.VMEM((t + 2 * HALO, CONV_WIDTH), F32),
                        pltpu.VMEM((SUBLANES, CONV_SPAN + SHIFT_ROWS, CONV_WIDTH), F32),
                        pltpu.VMEM((t, CONV_WIDTH), BF16),
                        pltpu.VMEM((CONV_KERNEL + 3, SUBLANES, CONV_WIDTH), F32)],
        compiler_params=pltpu.CompilerParams(
            dimension_semantics=("arbitrary",), vmem_limit_bytes=VMEM_LIMIT_BYTES),
        name="block_rest",
    )(x, q, k, v, sg, u, u, u, u, u, u, rf, lgf_lanes, lgb_lanes, gn_w, gn_b, conv_w, conv_b, ln_w, ln_b,
      w_out, xattn_norm_w, kt, vm, w_xq, w_xo, mlp_norm_w, w_ff1, w_ff2, final_norm_w)


def kernel(x, mem, positions, mix_norm_w, w_in, ret_decay_f, ret_decay_b, ret_gn_w, ret_gn_b,
           conv_w, conv_b, conv_ln_w, conv_ln_b, w_out, xattn_norm_w, mem_norm_w,
           w_xq, w_xkv, w_xo, mlp_norm_w, w_ff1, w_ff2, final_norm_w):
    assert w_in.shape[0] == 1, "single-layer block"
    b, s, d = x.shape
    half = HEAD_DIM // 2
    inv_freq = ROPE_THETA ** (-jnp.arange(half, dtype=F32) / half)
    invf2 = jnp.concatenate([inv_freq, inv_freq])[:, None]
    pos3 = positions.reshape(b, 1, s)
    row = lambda a: a.reshape(1, -1)
    lgf_lanes = jnp.repeat(jax.nn.log_sigmoid(ret_decay_f[0].astype(F32)), HEAD_DIM)[None]
    lgb_lanes = jnp.repeat(jax.nn.log_sigmoid(ret_decay_b[0].astype(F32)), HEAD_DIM)[None]
    kt, vm = _mem_kv(mem, row(mem_norm_w[0]), w_xkv[0])
    (q, k, v, sg, u, rf, w_out_b, w_xq_b, w_xo_b, w_ff1_b, w_ff2_b) = _in_proj(
        x, pos3, row(mix_norm_w[0]), invf2, lgf_lanes, w_in[0],
        (w_out[0], w_xq[0], w_xo[0], w_ff1[0], w_ff2[0]))
    return _block_rest(x, q, k, v, sg, u, rf, lgf_lanes, lgb_lanes,
                       row(ret_gn_w[0]), row(ret_gn_b[0]),
                       conv_w[0].reshape(CONV_KERNEL, CONV_WIDTH), row(conv_b[0]),
                       row(conv_ln_w[0]), row(conv_ln_b[0]),
                       w_out_b, row(xattn_norm_w[0]), kt, vm, w_xq_b, w_xo_b,
                       row(mlp_norm_w[0]), w_ff1_b, w_ff2_b, row(final_norm_w))
```

```python
import functools

import jax
import jax.numpy as jnp
import numpy as np
from jax import lax
from jax.experimental import pallas as pl
from jax.experimental.pallas import tpu as pltpu

F32 = jnp.float32
BF16 = jnp.bfloat16

D_MODEL = 1024
RET_WIDTH = 512
CONV_WIDTH = 512
RET_HEADS = 4
HEAD_DIM = 128
CHUNK = 128
CONV_KERNEL = 31
CONV_PAD = CONV_KERNEL // 2
HALO = 16
XATTN_HEADS = 4
XATTN_HEAD_DIM = 256
D_FF = 4096
ROPE_THETA = 10000.0
EPS = 1e-6

IN_PROJ_TILE = 1024
REST_TILE = 512
FF_CHUNK = 1024
CONV_ROWS = 32
CONV_SPAN = 256
CAST_ROWS = 128
HALF_PI_1 = 1.5703125
HALF_PI_2 = 4.837512969970703125e-4
HALF_PI_3 = 7.54978995489188216e-8
SUBLANES = 8
BF16_SUBLANES = 16
LANES = 128
SHIFT_ROWS = ((HALO - CONV_PAD + CONV_KERNEL - 1) // SUBLANES) * SUBLANES
VMEM_LIMIT_BYTES = 58 * 1024 * 1024


def _rms(x, w):
    ms = jnp.mean(x * x, axis=-1, keepdims=True)
    return x * lax.rsqrt(ms + EPS) * w


def _dot(a, b):
    return jnp.dot(a, b, preferred_element_type=F32)


def _dot_tn(a, b):
    return lax.dot_general(a, b, (((0,), (0,)), ((), ())), preferred_element_type=F32)


def _dot_nt(a, b):
    return lax.dot_general(a, b, (((1,), (1,)), ((), ())), preferred_element_type=F32)


def _block_diag(a):
    n, w2 = a.shape
    zero = jnp.zeros((n, w2 // 2), a.dtype)
    return jnp.concatenate([jnp.concatenate([a[:, :w2 // 2], zero], axis=1),
                            jnp.concatenate([zero, a[:, w2 // 2:]], axis=1)], axis=0)


def _const_spec(shape):
    nd = len(shape)
    return pl.BlockSpec(shape, lambda *_: (0,) * nd, pipeline_mode=pl.Buffered(1))


def _mem_kv_body(mem_ref, nw_ref, wkv_ref, kt_ref, v_ref):
    m = _rms(mem_ref[0], nw_ref[...]).astype(BF16)
    kv = _dot(m, wkv_ref[...].astype(BF16))
    k = kv[:, :D_MODEL] * (XATTN_HEAD_DIM ** -0.5)
    kt_ref[0] = k.T.astype(BF16)
    v_ref[0] = kv[:, D_MODEL:].astype(BF16)


def _mem_kv(mem, mem_norm_w, w_xkv):
    b, m, d = mem.shape
    return pl.pallas_call(
        _mem_kv_body,
        grid=(b,),
        in_specs=[pl.BlockSpec((1, m, d), lambda i: (i, 0, 0)),
                  _const_spec((1, d)),
                  _const_spec((d, 2 * d))],
        out_specs=[pl.BlockSpec((1, d, m), lambda i: (i, 0, 0)),
                   pl.BlockSpec((1, m, d), lambda i: (i, 0, 0))],
        out_shape=[jax.ShapeDtypeStruct((b, d, m), BF16),
                   jax.ShapeDtypeStruct((b, m, d), BF16)],
        compiler_params=pltpu.CompilerParams(
            dimension_semantics=("arbitrary",), vmem_limit_bytes=VMEM_LIMIT_BYTES),
        name="mem_kv",
    )(mem, mem_norm_w, w_xkv)


def _cos_or_sin(x, want_sin):
    k = jnp.floor(x * (2.0 / np.pi) + 0.5)
    r = ((x - k * HALF_PI_1) - k * HALF_PI_2) - k * HALF_PI_3
    r2 = r * r
    sin_r = r + r * r2 * (-1.6666654611e-1 + r2 * (8.3321608736e-3 + r2 * -1.9515295891e-4))
    cos_r = 1.0 - 0.5 * r2 + r2 * r2 * (4.166664568298827e-2
                                        + r2 * (-1.388731625493765e-3 + r2 * 2.443315711809948e-5))
    kq = k + jnp.where(want_sin, 3.0, 0.0)
    q = kq - 4.0 * jnp.floor(kq * 0.25)
    hi = jnp.floor(q * 0.5)
    lo = q - 2.0 * hi
    return (1.0 - 2.0 * hi) * (cos_r - lo * (cos_r + sin_r))


def _in_proj_body(x_ref, pos_ref, nw_ref, invf_ref, lgf_ref, w_ref,
                  wout_ref, wxq_ref, wxo_ref, w1_ref, w2_ref,
                  q_ref, k_ref, v_ref, sg_ref, u_ref, rf_ref,
                  wout_o, wxq_o, wxo_o, w1_o, w2_o, state_ref, wbf_ref):
    t = x_ref.shape[1]
    R = RET_WIDTH

    @pl.when((pl.program_id(0) == 0) & (pl.program_id(1) == 0))
    def _():
        for r0 in range(0, D_MODEL, CAST_ROWS):
            wbf_ref[r0:r0 + CAST_ROWS, :] = w_ref[r0:r0 + CAST_ROWS, :].astype(BF16)

    @pl.when(pl.program_id(1) == 0)
    def _():
        state_ref[...] = jnp.zeros_like(state_ref)

    for src, dst in ((wout_ref, wout_o), (wxq_ref, wxq_o), (wxo_ref, wxo_o), (w1_ref, w1_o), (w2_ref, w2_o)):
        dst[...] = src[...].astype(BF16)

    h = _rms(x_ref[0], nw_ref[...]).astype(BF16)

    ang_t = invf_ref[...] * pos_ref[0].astype(F32)
    cs = _cos_or_sin(ang_t, lax.broadcasted_iota(jnp.int32, (HEAD_DIM, 1), 0) >= HEAD_DIM // 2).T
    upper = lax.broadcasted_iota(jnp.int32, (1, HEAD_DIM), 1) >= HEAD_DIM // 2
    sc = pltpu.roll(cs, HEAD_DIM // 2, 1)
    cos2 = jnp.where(upper, sc, cs)
    sin2 = jnp.where(upper, cs, -sc)

    def rope(p):
        outs = []
        for hd in range(RET_HEADS):
            ph = p[:, hd * HEAD_DIM:(hd + 1) * HEAD_DIM]
            outs.append(ph * cos2 + pltpu.roll(ph, HEAD_DIM // 2, 1) * sin2)
        return jnp.concatenate(outs, axis=1)

    q = rope(_dot(h, wbf_ref[:, 0:R]))
    q_ref[0] = q.astype(BF16)
    k = rope(_dot(h, wbf_ref[:, R:2 * R])) * (HEAD_DIM ** -0.5)
    k_ref[0] = k.astype(BF16)
    v = _dot(h, wbf_ref[:, 2 * R:3 * R]).astype(BF16)
    v_ref[0] = v
    g = _dot(h, wbf_ref[:, 3 * R:4 * R])
    sg_ref[0] = (g * jax.nn.sigmoid(g)).astype(BF16)
    a = _dot(h, wbf_ref[:, 4 * R:5 * R])
    bb = _dot(h, wbf_ref[:, 5 * R:6 * R])
    u_ref[0] = (a * jax.nn.sigmoid(bb)).astype(BF16)

    lgf = lgf_ref[...]
    row = lax.broadcasted_iota(jnp.int32, (CHUNK, R), 0).astype(F32)
    zeta = jnp.exp(lgf * (CHUNK - 1.0 - row))
    chunk_decay = jnp.exp(lgf * float(CHUNK))
    state = state_ref[...]
    for c in range(t // CHUNK):
        rf_ref[0, c] = state.astype(BF16)
        rows = slice(c * CHUNK, (c + 1) * CHUNK)
        kz = (k[rows] * zeta).astype(BF16)
        vc = v[rows]
        upd = jnp.concatenate(
            [_dot_tn(kz[:, hd * HEAD_DIM:(hd + 1) * HEAD_DIM],
                     vc[:, hd * HEAD_DIM:(hd + 1) * HEAD_DIM]) for hd in range(RET_HEADS)], axis=1)
        state = chunk_decay * state + upd
    state_ref[...] = state


def _in_proj(x, pos3, mix_norm_w, invf2, lgf_lanes, w_in, later_weights):
    b, s, d = x.shape
    t = IN_PROJ_TILE
    R = RET_WIDTH
    n_j = s // t
    n_steps = b * n_j
    tok = lambda i, j: (i, j, 0)
    act_spec = pl.BlockSpec((1, t, R), tok)
    act_shape = jax.ShapeDtypeStruct((b, s, R), BF16)
    cast_specs = []
    for w in later_weights:
        rows, cols = w.shape
        assert rows % (n_steps * BF16_SUBLANES) == 0
        cast_specs.append(pl.BlockSpec((rows // n_steps, cols), lambda i, j: (i * n_j + j, 0)))
    return pl.pallas_call(
        _in_proj_body,
        grid=(b, n_j),
        in_specs=[pl.BlockSpec((1, t, d), tok),
                  pl.BlockSpec((1, 1, t), lambda i, j: (i, 0, j)),
                  _const_spec((1, d)),
                  _const_spec((HEAD_DIM, 1)),
                  _const_spec((1, R)),
                  _const_spec((d, 6 * R))] + cast_specs,
        out_specs=([act_spec] * 5 + [pl.BlockSpec((1, t // CHUNK, CHUNK, R), lambda i, j: (i, j, 0, 0))]
                   + cast_specs),
        out_shape=([act_shape] * 5 + [jax.ShapeDtypeStruct((b, s // CHUNK, CHUNK, R), BF16)]
                   + [jax.ShapeDtypeStruct(w.shape, BF16) for w in later_weights]),
        scratch_shapes=[pltpu.VMEM((CHUNK, R), F32),
                        pltpu.VMEM((d, 6 * R), BF16)],
        compiler_params=pltpu.CompilerParams(
            dimension_semantics=("arbitrary", "arbitrary"), vmem_limit_bytes=VMEM_LIMIT_BYTES),
        name="in_proj",
    )(x, pos3, mix_norm_w, invf2, lgf_lanes, w_in, *later_weights)


def _conv_pieces(u_ref, up_ref, un_ref, has_prev, has_next,
                 cpar_ref, uext_ref, ush_ref, yconv_ref):
    t = u_ref.shape[1]
    span = CONV_SPAN + SHIFT_ROWS
    groups = (CONV_ROWS // SUBLANES, SUBLANES, CONV_WIDTH)
    block_vregs = CONV_ROWS * CONV_WIDTH // (SUBLANES * LANES)
    span_vregs = span * CONV_WIDTH // (SUBLANES * LANES)
    accs = {}

    def fill(gate):
        del gate
        zero_halo = jnp.zeros((HALO, CONV_WIDTH), F32)
        uext_ref[0:HALO, :] = jnp.where(has_prev, up_ref[0].astype(F32), zero_halo)
        uext_ref[HALO:HALO + t, :] = u_ref[0].astype(F32)
        uext_ref[HALO + t:HALO + t + HALO, :] = jnp.where(has_next, un_ref[0].astype(F32), zero_halo)

    def shift(hb, sft, gate):
        ush_ref[sft, 0:span, :] = _gated(uext_ref[hb + sft:hb + sft + span, :], gate)

    def tap_step(hb, base, tap, gate):
        grp, sft = divmod(HALO - CONV_PAD + tap, SUBLANES)
        lo = base + grp * SUBLANES
        if sft == 0:
            src = uext_ref[hb + lo:hb + lo + CONV_ROWS, :]
        else:
            src = ush_ref[sft, lo:lo + CONV_ROWS, :]
        acc = accs.pop((hb, base)) if tap else jnp.broadcast_to(cpar_ref[CONV_KERNEL][None], groups)
        weight = cpar_ref[tap]
        weight = _gated(weight, gate)
        acc = acc + weight[None] * src.reshape(groups)
        if tap < CONV_KERNEL - 1:
            accs[(hb, base)] = acc
            return
        mu = jnp.mean(acc, axis=-1, keepdims=True)
        yc = acc - mu
        var = jnp.mean(yc * yc, axis=-1, keepdims=True)
        yl = yc * lax.rsqrt(var + EPS) * cpar_ref[CONV_KERNEL + 1][None] + cpar_ref[CONV_KERNEL + 2][None]
        out = (yl * jax.nn.sigmoid(yl)).reshape(CONV_ROWS, CONV_WIDTH)
        yconv_ref[hb + base:hb + base + CONV_ROWS, :] = out.astype(BF16)

    work = [(2 * (t + 2 * HALO) * CONV_WIDTH // (SUBLANES * LANES), fill)]
    for hb in range(0, t, CONV_SPAN):
        for sft in range(1, SUBLANES):
            work.append((3 * span_vregs, functools.partial(shift, hb, sft)))
        for base in range(0, CONV_SPAN, CONV_ROWS):
            for tap in range(CONV_KERNEL):
                cost = 2 * block_vregs + (12 * block_vregs if tap == CONV_KERNEL - 1 else 0)
                work.append((cost, functools.partial(tap_step, hb, base, tap)))
    return work


def _gated(x, gate):
    if gate is None:
        return x
    rows, cols = x.shape
    return x + jnp.tile(gate, (rows // SUBLANES, cols // LANES))


def _gates_from(y, n):
    rows, cols = y.shape
    tiles = [(r, c) for c in range(0, cols, LANES) for r in range(0, rows, SUBLANES)]
    picks = [tiles[(i * len(tiles)) // n] for i in range(n)]
    return [pltpu.bitcast((pltpu.bitcast(y[r:r + SUBLANES, c:c + LANES], jnp.uint32) >> 16) >> 16, F32)
            for r, c in picks]


def _rest_body(n_tiles,
               x_ref, q_ref, k_ref, v_ref, sg_ref, u_ref, up_ref, un_ref, u1_ref, up1_ref, un1_ref, rf_ref,
               lgf_ref, lgb_ref, gnw_ref, gnb_ref, cw_ref, cb_ref, lnw_ref, lnb_ref,
               wout_ref, xnw_ref, kt_ref, vm_ref, wxq_ref, wxo_ref,
               mnw_ref, w1_ref, w2_ref, fnw_ref,
               o_ref, rb_ref, mix_ref, uext_ref, ush_ref, yconv_ref, cpar_ref):
    t = x_ref.shape[1]
    R = RET_WIDTH
    step = pl.program_id(0)
    last_step = pl.num_programs(0) - 1
    tile = n_tiles - 1 - step % n_tiles
    step1 = jnp.minimum(step + 1, last_step)
    tile1 = n_tiles - 1 - step1 % n_tiles
    conv_consts = (cpar_ref, uext_ref, ush_ref, yconv_ref)

    @pl.when(step % n_tiles == 0)
    def _():
        rb_ref[...] = jnp.zeros_like(rb_ref)

    @pl.when(step == 0)
    def _():
        for tap in range(CONV_KERNEL):
            cpar_ref[tap] = jnp.broadcast_to(cw_ref[tap:tap + 1, :], (SUBLANES, CONV_WIDTH))
        for i, par in enumerate((cb_ref, lnw_ref, lnb_ref)):
            cpar_ref[CONV_KERNEL + i] = jnp.broadcast_to(par[...], (SUBLANES, CONV_WIDTH))
        for _, item in _conv_pieces(u_ref, up_ref, un_ref, tile > 0, tile < n_tiles - 1, *conv_consts):
            item(None)

    mix_ref[:, R:R + CONV_WIDTH] = yconv_ref[...]

    lgf = lgf_ref[...]
    lgb = lgb_ref[...]
    row = lax.broadcasted_iota(jnp.int32, (CHUNK, R), 0).astype(F32)
    xi_f = jnp.exp(lgf * (row + 1.0))
    xi_b = jnp.exp(lgb * (float(CHUNK) - row))
    zeta_b = jnp.exp(lgb * row)
    decay_b = jnp.exp(lgb * float(CHUNK))
    ii = lax.broadcasted_iota(jnp.int32, (CHUNK, CHUNK), 0)
    jj = lax.broadcasted_iota(jnp.int32, (CHUNK, CHUNK), 1)
    diff = (ii - jj).astype(F32)
    dmask = jnp.concatenate(
        [jnp.where(diff >= 0.0,
                   jnp.exp(lgf[:, hd * HEAD_DIM:(hd + 1) * HEAD_DIM] * jnp.maximum(diff, 0.0)),
                   jnp.exp(lgb[:, hd * HEAD_DIM:(hd + 1) * HEAD_DIM] * jnp.maximum(-diff, 0.0)))
         for hd in range(RET_HEADS)], axis=1)

    gnw = gnw_ref[...]
    gnb = gnb_ref[...]
    rb = rb_ref[...]
    for c in reversed(range(t // CHUNK)):
        rows = slice(c * CHUNK, (c + 1) * CHUNK)
        qc = q_ref[0, rows, :]
        kc = k_ref[0, rows, :]
        vc = v_ref[0, rows, :]
        sgc = sg_ref[0, rows, :].astype(F32)
        rfc = rf_ref[0, c]
        rbb = rb.astype(BF16)
        qf = qc.astype(F32)
        qxf = (qf * xi_f).astype(BF16)
        qxb = (qf * xi_b).astype(BF16)
        kzb = (kc.astype(F32) * zeta_b).astype(BF16)
        ys = []
        upds = []
        for ps in (slice(0, 2 * HEAD_DIM), slice(2 * HEAD_DIM, 4 * HEAD_DIM)):
            p = (_dot_nt(qc[:, ps], _block_diag(kc[:, ps])) * dmask[:, ps]).astype(BF16)
            lhs = jnp.concatenate([p, qxf[:, ps], qxb[:, ps]], axis=1)
            rhs = jnp.concatenate([_block_diag(vc[:, ps]), _block_diag(rfc[:, ps]), _block_diag(rbb[:, ps])],
                                  axis=0)
            ys.append(_dot(lhs, rhs))
            kv = _dot_tn(kzb[:, ps], vc[:, ps])
            upds += [kv[:HEAD_DIM, :HEAD_DIM], kv[HEAD_DIM:, HEAD_DIM:]]
        yn = []
        for hd, y in enumerate(ys):
            for half in (y[:, :HEAD_DIM], y[:, HEAD_DIM:]):
                mu = jnp.mean(half, axis=-1, keepdims=True)
                yc = half - mu
                var = jnp.mean(yc * yc, axis=-1, keepdims=True)
                yn.append(yc * lax.rsqrt(var + EPS))
        yn = jnp.concatenate(yn, axis=1) * gnw + gnb
        mix_ref[rows, 0:R] = (sgc * yn).astype(BF16)
        rb = decay_b * rb + jnp.concatenate(upds, axis=1)
    rb_ref[...] = rb

    work = _conv_pieces(u1_ref, up1_ref, un1_ref, tile1 > 0, tile1 < n_tiles - 1, *conv_consts)
    total_cost = sum(cost for cost, _ in work)
    n_chunks = D_FF // FF_CHUNK
    n_shares = 3 + 2 * n_chunks - 1
    progress = {"share": 0, "item": 0, "cost": 0}

    def conv_share(result):
        progress["share"] += 1
        items = []
        while progress["item"] < len(work) and progress["cost"] * n_shares < total_cost * progress["share"]:
            cost, item = work[progress["item"]]
            items.append(item)
            progress["item"] += 1
            progress["cost"] += cost
        for item, gate in zip(items, _gates_from(result, len(items))):
            item(gate)
        return result

    x1 = x_ref[0] + conv_share(_dot(mix_ref[...], wout_ref[...]))

    h2 = _rms(x1, xnw_ref[...]).astype(BF16)
    qx = conv_share(_dot(h2, wxq_ref[...])).astype(BF16)
    outs = []
    for hd in range(XATTN_HEADS):
        hs = slice(hd * XATTN_HEAD_DIM, (hd + 1) * XATTN_HEAD_DIM)
        s = _dot(qx[:, hs], kt_ref[0, hs, :])
        e = jnp.exp(s - jnp.max(s, axis=-1, keepdims=True))
        p = e * (1.0 / jnp.sum(e, axis=-1, keepdims=True))
        outs.append(_dot(p.astype(BF16), vm_ref[0, :, hs]).astype(BF16))
    x2 = x1 + conv_share(_dot(jnp.concatenate(outs, axis=1), wxo_ref[...]))

    hm = _rms(x2, mnw_ref[...]).astype(BF16)
    acc = jnp.zeros((t, D_MODEL), F32)
    for c in range(n_chunks):
        cs = slice(c * FF_CHUNK, (c + 1) * FF_CHUNK)
        hc = conv_share(jnp.maximum(_dot(hm, w1_ref[:, cs]), 0.0))
        part = _dot((hc * hc).astype(BF16), w2_ref[cs, :])
        acc = acc + (conv_share(part) if c < n_chunks - 1 else part)
    assert progress["item"] == len(work) and progress["share"] == n_shares
    o_ref[0] = _rms(x2 + acc, fnw_ref[...])


def _block_rest(x, q, k, v, sg, u, rf, lgf_lanes, lgb_lanes, gn_w, gn_b, conv_w, conv_b, ln_w, ln_b,
                w_out, xattn_norm_w, kt, vm, w_xq, w_xo, mlp_norm_w, w_ff1, w_ff2, final_norm_w):
    b, s, d = x.shape
    t = REST_TILE
    R = RET_WIDTH
    n_tiles = s // t
    n_steps = b * n_tiles
    hb = t // HALO
    n_hb = s // HALO

    def batch_tile(step):
        return step // n_tiles, n_tiles - 1 - step % n_tiles

    def ahead(step):
        return jnp.minimum(step + 1, n_steps - 1)

    def main_map(step):
        bi, ti = batch_tile(step)
        return bi, ti, 0

    def prev_halo_map(step):
        bi, ti = batch_tile(step)
        return bi, jnp.maximum(ti * hb - 1, 0), 0

    def next_halo_map(step):
        bi, ti = batch_tile(step)
        return bi, jnp.minimum((ti + 1) * hb, n_hb - 1), 0

    act_spec = pl.BlockSpec((1, t, R), main_map)
    halo_shape = (1, HALO, R)
    m = kt.shape[2]
    return pl.pallas_call(
        functools.partial(_rest_body, n_tiles),
        grid=(n_steps,),
        in_specs=[pl.BlockSpec((1, t, d), main_map),
                  act_spec, act_spec, act_spec, act_spec, act_spec,
                  pl.BlockSpec(halo_shape, prev_halo_map),
                  pl.BlockSpec(halo_shape, next_halo_map),
                  pl.BlockSpec((1, t, R), lambda st: main_map(ahead(st))),
                  pl.BlockSpec(halo_shape, lambda st: prev_halo_map(ahead(st))),
                  pl.BlockSpec(halo_shape, lambda st: next_halo_map(ahead(st))),
                  pl.BlockSpec((1, t // CHUNK, CHUNK, R), lambda st: main_map(st) + (0,)),
                  _const_spec((1, R)), _const_spec((1, R)),
                  _const_spec((1, R)), _const_spec((1, R)),
                  _const_spec((CONV_KERNEL, CONV_WIDTH)),
                  _const_spec((1, CONV_WIDTH)), _const_spec((1, CONV_WIDTH)), _const_spec((1, CONV_WIDTH)),
                  _const_spec((d, d)),
                  _const_spec((1, d)),
                  pl.BlockSpec((1, d, m), lambda st: (st // n_tiles, 0, 0)),
                  pl.BlockSpec((1, m, d), lambda st: (st // n_tiles, 0, 0)),
                  _const_spec((d, d)), _const_spec((d, d)),
                  _const_spec((1, d)),
                  _const_spec((d, D_FF)), _const_spec((D_FF, d)),
                  _const_spec((1, d))],
        out_specs=pl.BlockSpec((1, t, d), main_map),
        out_shape=jax.ShapeDtypeStruct((b, s, d), F32),
        scratch_shapes=[pltpu.VMEM((CHUNK, R), F32),
                        pltpu.VMEM((t, D_MODEL), BF16),
                        pltpu.VMEM((t + 2 * HALO, CONV_WIDTH), F32),
                        pltpu.VMEM((SUBLANES, CONV_SPAN + SHIFT_ROWS, CONV_WIDTH), F32),
                        pltpu.VMEM((t, CONV_WIDTH), BF16),
                        pltpu.VMEM((CONV_KERNEL + 3, SUBLANES, CONV_WIDTH), F32)],
        compiler_params=pltpu.CompilerParams(
            dimension_semantics=("arbitrary",), vmem_limit_bytes=VMEM_LIMIT_BYTES),
        name="block_rest",
    )(x, q, k, v, sg, u, u, u, u, u, u, rf, lgf_lanes, lgb_lanes, gn_w, gn_b, conv_w, conv_b, ln_w, ln_b,
      w_out, xattn_norm_w, kt, vm, w_xq, w_xo, mlp_norm_w, w_ff1, w_ff2, final_norm_w)


def kernel(x, mem, positions, mix_norm_w, w_in, ret_decay_f, ret_decay_b, ret_gn_w, ret_gn_b,
           conv_w, conv_b, conv_ln_w, conv_ln_b, w_out, xattn_norm_w, mem_norm_w,
           w_xq, w_xkv, w_xo, mlp_norm_w, w_ff1, w_ff2, final_norm_w):
    assert w_in.shape[0] == 1, "single-layer block"
    b, s, d = x.shape
    half = HEAD_DIM // 2
    inv_freq = ROPE_THETA ** (-jnp.arange(half, dtype=F32) / half)
    invf2 = jnp.concatenate([inv_freq, inv_freq])[:, None]
    pos3 = positions.reshape(b, 1, s)
    row = lambda a: a.reshape(1, -1)
    lgf_lanes = jnp.repeat(jax.nn.log_sigmoid(ret_decay_f[0].astype(F32)), HEAD_DIM)[None]
    lgb_lanes = jnp.repeat(jax.nn.log_sigmoid(ret_decay_b[0].astype(F32)), HEAD_DIM)[None]
    kt, vm = _mem_kv(mem, row(mem_norm_w[0]), w_xkv[0])
    (q, k, v, sg, u, rf, w_out_b, w_xq_b, w_xo_b, w_ff1_b, w_ff2_b) = _in_proj(
        x, pos3, row(mix_norm_w[0]), invf2, lgf_lanes, w_in[0],
        (w_out[0], w_xq[0], w_xo[0], w_ff1[0], w_ff2[0]))
    return _block_rest(x, q, k, v, sg, u, rf, lgf_lanes, lgb_lanes,
                       row(ret_gn_w[0]), row(ret_gn_b[0]),
                       conv_w[0].reshape(CONV_KERNEL, CONV_WIDTH), row(conv_b[0]),
                       row(conv_ln_w[0]), row(conv_ln_b[0]),
                       w_out_b, row(xattn_norm_w[0]), kt, vm, w_xq_b, w_xo_b,
                       row(mlp_norm_w[0]), w_ff1_b, w_ff2_b, row(final_norm_w))
```

```python
import functools

import jax
import jax.numpy as jnp
import numpy as np
from jax import lax
from jax.experimental import pallas as pl
from jax.experimental.pallas import tpu as pltpu

F32 = jnp.float32
BF16 = jnp.bfloat16

D_MODEL = 1024
RET_WIDTH = 512
CONV_WIDTH = 512
RET_HEADS = 4
HEAD_DIM = 128
CHUNK = 128
CONV_KERNEL = 31
CONV_PAD = CONV_KERNEL // 2
HALO = 16
XATTN_HEADS = 4
XATTN_HEAD_DIM = 256
D_FF = 4096
ROPE_THETA = 10000.0
EPS = 1e-6

IN_PROJ_TILE = 1024
REST_TILE = 512
FF_CHUNK = 1024
CONV_ROWS = 32
CONV_SPAN = 128
CAST_ROWS = 128
HALF_PI_1 = 1.5703125
HALF_PI_2 = 4.837512969970703125e-4
HALF_PI_3 = 7.54978995489188216e-8
SUBLANES = 8
BF16_SUBLANES = 16
LANES = 128
SHIFT_ROWS = ((HALO - CONV_PAD + CONV_KERNEL - 1) // SUBLANES) * SUBLANES
VMEM_LIMIT_BYTES = 58 * 1024 * 1024


def _inv_rms(x):
    return lax.rsqrt(jnp.mean(x * x, axis=-1, keepdims=True) + EPS)


def _rms(x, w):
    return x * _inv_rms(x) * w


def _dot(a, b):
    return jnp.dot(a, b, preferred_element_type=F32)


def _dot_tn(a, b):
    return lax.dot_general(a, b, (((0,), (0,)), ((), ())), preferred_element_type=F32)


def _dot_nt(a, b):
    return lax.dot_general(a, b, (((1,), (1,)), ((), ())), preferred_element_type=F32)


def _block_diag(a):
    n, w2 = a.shape
    zero = jnp.zeros((n, w2 // 2), a.dtype)
    return jnp.concatenate([jnp.concatenate([a[:, :w2 // 2], zero], axis=1),
                            jnp.concatenate([zero, a[:, w2 // 2:]], axis=1)], axis=0)


def _const_spec(shape):
    nd = len(shape)
    return pl.BlockSpec(shape, lambda *_: (0,) * nd, pipeline_mode=pl.Buffered(1))


def _mem_kv_body(mem_ref, nw_ref, wkv_ref, kt_ref, v_ref):
    m = _rms(mem_ref[0], nw_ref[...]).astype(BF16)
    kv = _dot(m, wkv_ref[...].astype(BF16))
    k = kv[:, :D_MODEL] * (XATTN_HEAD_DIM ** -0.5)
    kt_ref[0] = k.T.astype(BF16)
    v_ref[0] = kv[:, D_MODEL:].astype(BF16)


def _mem_kv(mem, mem_norm_w, w_xkv):
    b, m, d = mem.shape
    return pl.pallas_call(
        _mem_kv_body,
        grid=(b,),
        in_specs=[pl.BlockSpec((1, m, d), lambda i: (i, 0, 0)),
                  _const_spec((1, d)),
                  _const_spec((d, 2 * d))],
        out_specs=[pl.BlockSpec((1, d, m), lambda i: (i, 0, 0)),
                   pl.BlockSpec((1, m, d), lambda i: (i, 0, 0))],
        out_shape=[jax.ShapeDtypeStruct((b, d, m), BF16),
                   jax.ShapeDtypeStruct((b, m, d), BF16)],
        compiler_params=pltpu.CompilerParams(
            dimension_semantics=("arbitrary",), vmem_limit_bytes=VMEM_LIMIT_BYTES),
        name="mem_kv",
    )(mem, mem_norm_w, w_xkv)


def _cos_or_sin(x, want_sin):
    k = jnp.floor(x * (2.0 / np.pi) + 0.5)
    r = ((x - k * HALF_PI_1) - k * HALF_PI_2) - k * HALF_PI_3
    r2 = r * r
    sin_r = r + r * r2 * (-1.6666654611e-1 + r2 * (8.3321608736e-3 + r2 * -1.9515295891e-4))
    cos_r = 1.0 - 0.5 * r2 + r2 * r2 * (4.166664568298827e-2
                                        + r2 * (-1.388731625493765e-3 + r2 * 2.443315711809948e-5))
    kq = k + jnp.where(want_sin, 3.0, 0.0)
    q = kq - 4.0 * jnp.floor(kq * 0.25)
    hi = jnp.floor(q * 0.5)
    lo = q - 2.0 * hi
    return (1.0 - 2.0 * hi) * (cos_r - lo * (cos_r + sin_r))


def _in_proj_body(x_ref, pos_ref, nw_ref, invf_ref, lgf_ref, w_ref,
                  wout_ref, wxq_ref, wxo_ref, w1_ref, w2_ref,
                  q_ref, k_ref, v_ref, sg_ref, u_ref, rf_ref,
                  wout_o, wxq_o, wxo_o, w1_o, w2_o, state_ref, wbf_ref):
    t = x_ref.shape[1]
    R = RET_WIDTH

    @pl.when((pl.program_id(0) == 0) & (pl.program_id(1) == 0))
    def _():
        for r0 in range(0, D_MODEL, CAST_ROWS):
            wbf_ref[r0:r0 + CAST_ROWS, :] = w_ref[r0:r0 + CAST_ROWS, :].astype(BF16)

    @pl.when(pl.program_id(1) == 0)
    def _():
        state_ref[...] = jnp.zeros_like(state_ref)

    for src, dst in ((wout_ref, wout_o), (wxq_ref, wxq_o), (wxo_ref, wxo_o), (w1_ref, w1_o), (w2_ref, w2_o)):
        dst[...] = src[...].astype(BF16)

    h = _rms(x_ref[0], nw_ref[...]).astype(BF16)

    ang_t = invf_ref[...] * pos_ref[0].astype(F32)
    cs = _cos_or_sin(ang_t, lax.broadcasted_iota(jnp.int32, (HEAD_DIM, 1), 0) >= HEAD_DIM // 2).T
    upper = lax.broadcasted_iota(jnp.int32, (1, HEAD_DIM), 1) >= HEAD_DIM // 2
    sc = pltpu.roll(cs, HEAD_DIM // 2, 1)
    cos2 = jnp.where(upper, sc, cs)
    sin2 = jnp.where(upper, cs, -sc)

    def rope(p):
        outs = []
        for hd in range(RET_HEADS):
            ph = p[:, hd * HEAD_DIM:(hd + 1) * HEAD_DIM]
            outs.append(ph * cos2 + pltpu.roll(ph, HEAD_DIM // 2, 1) * sin2)
        return jnp.concatenate(outs, axis=1)

    q = rope(_dot(h, wbf_ref[:, 0:R]))
    q_ref[0] = q.astype(BF16)
    k = rope(_dot(h, wbf_ref[:, R:2 * R])) * (HEAD_DIM ** -0.5)
    k_ref[0] = k.astype(BF16)
    v = _dot(h, wbf_ref[:, 2 * R:3 * R]).astype(BF16)
    v_ref[0] = v
    g = _dot(h, wbf_ref[:, 3 * R:4 * R])
    sg_ref[0] = (g * jax.nn.sigmoid(g)).astype(BF16)
    a = _dot(h, wbf_ref[:, 4 * R:5 * R])
    bb = _dot(h, wbf_ref[:, 5 * R:6 * R])
    u_ref[0] = (a * jax.nn.sigmoid(bb)).astype(BF16)

    lgf = lgf_ref[...]
    row = lax.broadcasted_iota(jnp.int32, (CHUNK, R), 0).astype(F32)
    zeta = jnp.exp(lgf * (CHUNK - 1.0 - row))
    chunk_decay = jnp.exp(lgf * float(CHUNK))
    state = state_ref[...]
    for c in range(t // CHUNK):
        rf_ref[0, c] = state.astype(BF16)
        rows = slice(c * CHUNK, (c + 1) * CHUNK)
        kz = (k[rows] * zeta).astype(BF16)
        vc = v[rows]
        upd = jnp.concatenate(
            [_dot_tn(kz[:, hd * HEAD_DIM:(hd + 1) * HEAD_DIM],
                     vc[:, hd * HEAD_DIM:(hd + 1) * HEAD_DIM]) for hd in range(RET_HEADS)], axis=1)
        state = chunk_decay * state + upd
    state_ref[...] = state


def _in_proj(x, pos3, mix_norm_w, invf2, lgf_lanes, w_in, later_weights):
    b, s, d = x.shape
    t = IN_PROJ_TILE
    R = RET_WIDTH
    n_j = s // t
    n_steps = b * n_j
    tok = lambda i, j: (i, j, 0)
    act_spec = pl.BlockSpec((1, t, R), tok)
    act_shape = jax.ShapeDtypeStruct((b, s, R), BF16)
    cast_specs = []
    for w in later_weights:
        rows, cols = w.shape
        assert rows % (n_steps * BF16_SUBLANES) == 0
        cast_specs.append(pl.BlockSpec((rows // n_steps, cols), lambda i, j: (i * n_j + j, 0)))
    return pl.pallas_call(
        _in_proj_body,
        grid=(b, n_j),
        in_specs=[pl.BlockSpec((1, t, d), tok),
                  pl.BlockSpec((1, 1, t), lambda i, j: (i, 0, j)),
                  _const_spec((1, d)),
                  _const_spec((HEAD_DIM, 1)),
                  _const_spec((1, R)),
                  _const_spec((d, 6 * R))] + cast_specs,
        out_specs=([act_spec] * 5 + [pl.BlockSpec((1, t // CHUNK, CHUNK, R), lambda i, j: (i, j, 0, 0))]
                   + cast_specs),
        out_shape=([act_shape] * 5 + [jax.ShapeDtypeStruct((b, s // CHUNK, CHUNK, R), BF16)]
                   + [jax.ShapeDtypeStruct(w.shape, BF16) for w in later_weights]),
        scratch_shapes=[pltpu.VMEM((CHUNK, R), F32),
                        pltpu.VMEM((d, 6 * R), BF16)],
        compiler_params=pltpu.CompilerParams(
            dimension_semantics=("arbitrary", "arbitrary"), vmem_limit_bytes=VMEM_LIMIT_BYTES),
        name="in_proj",
    )(x, pos3, mix_norm_w, invf2, lgf_lanes, w_in, *later_weights)


def _conv_pieces(u_ref, up_ref, un_ref, has_prev, has_next,
                 cpar_ref, uext_ref, ush_ref, yconv_ref):
    t = u_ref.shape[1]
    span = CONV_SPAN + SHIFT_ROWS
    groups = (CONV_ROWS // SUBLANES, SUBLANES, CONV_WIDTH)
    block_vregs = CONV_ROWS * CONV_WIDTH // (SUBLANES * LANES)
    span_vregs = span * CONV_WIDTH // (SUBLANES * LANES)
    accs = {}

    def fill(gate):
        del gate
        zero_halo = jnp.zeros((HALO, CONV_WIDTH), F32)
        uext_ref[0:HALO, :] = jnp.where(has_prev, up_ref[0].astype(F32), zero_halo)
        uext_ref[HALO:HALO + t, :] = u_ref[0].astype(F32)
        uext_ref[HALO + t:HALO + t + HALO, :] = jnp.where(has_next, un_ref[0].astype(F32), zero_halo)

    def shift(hb, sft, gate):
        ush_ref[(hb // CONV_SPAN) % 2, sft, 0:span, :] = _gated(uext_ref[hb + sft:hb + sft + span, :], gate)

    def tap_step(hb, base, tap, gate):
        grp, sft = divmod(HALO - CONV_PAD + tap, SUBLANES)
        lo = base + grp * SUBLANES
        if sft == 0:
            src = uext_ref[hb + lo:hb + lo + CONV_ROWS, :]
        else:
            src = ush_ref[(hb // CONV_SPAN) % 2, sft, lo:lo + CONV_ROWS, :]
        acc = accs.pop((hb, base)) if tap else jnp.broadcast_to(cpar_ref[CONV_KERNEL][None], groups)
        weight = cpar_ref[tap]
        weight = _gated(weight, gate)
        acc = acc + weight[None] * src.reshape(groups)
        if tap < CONV_KERNEL - 1:
            accs[(hb, base)] = acc
            return
        mu = jnp.mean(acc, axis=-1, keepdims=True)
        yc = acc - mu
        var = jnp.mean(yc * yc, axis=-1, keepdims=True)
        yl = yc * lax.rsqrt(var + EPS) * cpar_ref[CONV_KERNEL + 1][None] + cpar_ref[CONV_KERNEL + 2][None]
        out = (yl * jax.nn.sigmoid(yl)).reshape(CONV_ROWS, CONV_WIDTH)
        yconv_ref[hb + base:hb + base + CONV_ROWS, :] = out.astype(BF16)

    def copies(hb):
        return [(3 * span_vregs, functools.partial(shift, hb, sft)) for sft in range(1, SUBLANES)]

    def taps(hb):
        return [(2 * block_vregs + (12 * block_vregs if tap == CONV_KERNEL - 1 else 0),
                 functools.partial(tap_step, hb, base, tap))
                for base in range(0, CONV_SPAN, CONV_ROWS) for tap in range(CONV_KERNEL)]

    work = [(2 * (t + 2 * HALO) * CONV_WIDTH // (SUBLANES * LANES), fill)] + copies(0)
    for hb in range(0, t, CONV_SPAN):
        ahead = copies(hb + CONV_SPAN) if hb + CONV_SPAN < t else []
        steps = taps(hb)
        every = len(steps) // (len(ahead) + 1)
        for i, step in enumerate(steps):
            work.append(step)
            if ahead and (i + 1) % every == 0:
                work.append(ahead.pop(0))
        assert not ahead
    return work


def _gated(x, gate):
    if gate is None:
        return x
    rows, cols = x.shape
    return x + jnp.tile(gate, (rows // SUBLANES, cols // LANES))


def _gates_from(y, n):
    rows, cols = y.shape
    tiles = [(r, c) for c in range(0, cols, LANES) for r in range(0, rows, SUBLANES)]
    picks = [tiles[(i * len(tiles)) // n] for i in range(n)]
    return [pltpu.bitcast((pltpu.bitcast(y[r:r + SUBLANES, c:c + LANES], jnp.uint32) >> 16) >> 16, F32)
            for r, c in picks]


def _rest_body(n_tiles,
               x_ref, q_ref, k_ref, v_ref, sg_ref, u_ref, up_ref, un_ref, u1_ref, up1_ref, un1_ref, rf_ref,
               lgf_ref, lgb_ref, gnw_ref, gnb_ref, cw_ref, cb_ref, lnw_ref, lnb_ref,
               wout_ref, xnw_ref, kt_ref, vm_ref, wxq_ref, wxo_ref,
               mnw_ref, w1_ref, w2_ref, fnw_ref,
               o_ref, rb_ref, mix_ref, uext_ref, ush_ref, yconv_ref, cpar_ref):
    t = x_ref.shape[1]
    R = RET_WIDTH
    step = pl.program_id(0)
    last_step = pl.num_programs(0) - 1
    tile = n_tiles - 1 - step % n_tiles
    step1 = jnp.minimum(step + 1, last_step)
    tile1 = n_tiles - 1 - step1 % n_tiles
    conv_consts = (cpar_ref, uext_ref, ush_ref, yconv_ref)

    @pl.when(step % n_tiles == 0)
    def _():
        rb_ref[...] = jnp.zeros_like(rb_ref)

    @pl.when(step == 0)
    def _():
        for tap in range(CONV_KERNEL):
            cpar_ref[tap] = jnp.broadcast_to(cw_ref[tap:tap + 1, :], (SUBLANES, CONV_WIDTH))
        for i, par in enumerate((cb_ref, lnw_ref, lnb_ref)):
            cpar_ref[CONV_KERNEL + i] = jnp.broadcast_to(par[...], (SUBLANES, CONV_WIDTH))
        for _, item in _conv_pieces(u_ref, up_ref, un_ref, tile > 0, tile < n_tiles - 1, *conv_consts):
            item(None)

    mix_ref[:, R:R + CONV_WIDTH] = yconv_ref[...]

    lgf = lgf_ref[...]
    lgb = lgb_ref[...]
    row = lax.broadcasted_iota(jnp.int32, (CHUNK, R), 0).astype(F32)
    xi_f = jnp.exp(lgf * (row + 1.0))
    xi_b = jnp.exp(lgb * (float(CHUNK) - row))
    zeta_b = jnp.exp(lgb * row)
    decay_b = jnp.exp(lgb * float(CHUNK))
    ii = lax.broadcasted_iota(jnp.int32, (CHUNK, CHUNK), 0)
    jj = lax.broadcasted_iota(jnp.int32, (CHUNK, CHUNK), 1)
    diff = (ii - jj).astype(F32)
    dmask = jnp.concatenate(
        [jnp.where(diff >= 0.0,
                   jnp.exp(lgf[:, hd * HEAD_DIM:(hd + 1) * HEAD_DIM] * jnp.maximum(diff, 0.0)),
                   jnp.exp(lgb[:, hd * HEAD_DIM:(hd + 1) * HEAD_DIM] * jnp.maximum(-diff, 0.0)))
         for hd in range(RET_HEADS)], axis=1)

    gnw = gnw_ref[...]
    gnb = gnb_ref[...]
    rb = rb_ref[...]
    for c in reversed(range(t // CHUNK)):
        rows = slice(c * CHUNK, (c + 1) * CHUNK)
        qc = q_ref[0, rows, :]
        kc = k_ref[0, rows, :]
        vc = v_ref[0, rows, :]
        sgc = sg_ref[0, rows, :].astype(F32)
        rfc = rf_ref[0, c]
        rbb = rb.astype(BF16)
        qf = qc.astype(F32)
        qxf = (qf * xi_f).astype(BF16)
        qxb = (qf * xi_b).astype(BF16)
        kzb = (kc.astype(F32) * zeta_b).astype(BF16)
        ys = []
        upds = []
        for ps in (slice(0, 2 * HEAD_DIM), slice(2 * HEAD_DIM, 4 * HEAD_DIM)):
            p = (_dot_nt(qc[:, ps], _block_diag(kc[:, ps])) * dmask[:, ps]).astype(BF16)
            lhs = jnp.concatenate([p, qxf[:, ps], qxb[:, ps]], axis=1)
            rhs = jnp.concatenate([_block_diag(vc[:, ps]), _block_diag(rfc[:, ps]), _block_diag(rbb[:, ps])],
                                  axis=0)
            ys.append(_dot(lhs, rhs))
            kv = _dot_tn(kzb[:, ps], vc[:, ps])
            upds += [kv[:HEAD_DIM, :HEAD_DIM], kv[HEAD_DIM:, HEAD_DIM:]]
        yn = []
        for hd, y in enumerate(ys):
            for half in (y[:, :HEAD_DIM], y[:, HEAD_DIM:]):
                mu = jnp.mean(half, axis=-1, keepdims=True)
                yc = half - mu
                var = jnp.mean(yc * yc, axis=-1, keepdims=True)
                yn.append(yc * lax.rsqrt(var + EPS))
        yn = jnp.concatenate(yn, axis=1) * gnw + gnb
        mix_ref[rows, 0:R] = (sgc * yn).astype(BF16)
        rb = decay_b * rb + jnp.concatenate(upds, axis=1)
    rb_ref[...] = rb

    work = _conv_pieces(u1_ref, up1_ref, un1_ref, tile1 > 0, tile1 < n_tiles - 1, *conv_consts)
    total_cost = sum(cost for cost, _ in work)
    n_chunks = D_FF // FF_CHUNK
    n_shares = 3 + 2 * n_chunks - 1
    progress = {"share": 0, "item": 0, "cost": 0}

    def conv_share(result):
        progress["share"] += 1
        items = []
        while progress["item"] < len(work) and progress["cost"] * n_shares < total_cost * progress["share"]:
            cost, item = work[progress["item"]]
            items.append(item)
            progress["item"] += 1
            progress["cost"] += cost
        for item, gate in zip(items, _gates_from(result, len(items))):
            item(gate)
        return result

    x1 = x_ref[0] + conv_share(_dot(mix_ref[...], wout_ref[...]))

    qx = _dot((x1 * xnw_ref[...]).astype(BF16), wxq_ref[...])
    qx = (conv_share(qx) * _inv_rms(x1)).astype(BF16)
    outs = []
    for hd in range(XATTN_HEADS):
        hs = slice(hd * XATTN_HEAD_DIM, (hd + 1) * XATTN_HEAD_DIM)
        s = _dot(qx[:, hs], kt_ref[0, hs, :])
        e = jnp.exp(s - jnp.max(s, axis=-1, keepdims=True))
        p = e * (1.0 / jnp.sum(e, axis=-1, keepdims=True))
        outs.append(_dot(p.astype(BF16), vm_ref[0, :, hs]).astype(BF16))
    x2 = x1 + conv_share(_dot(jnp.concatenate(outs, axis=1), wxo_ref[...]))

    hm = (x2 * mnw_ref[...]).astype(BF16)
    acc = jnp.zeros((t, D_MODEL), F32)
    for c in range(n_chunks):
        cs = slice(c * FF_CHUNK, (c + 1) * FF_CHUNK)
        hc = conv_share(jnp.maximum(_dot(hm, w1_ref[:, cs]), 0.0))
        part = _dot((hc * hc).astype(BF16), w2_ref[cs, :])
        acc = acc + (conv_share(part) if c < n_chunks - 1 else part)
    assert progress["item"] == len(work) and progress["share"] == n_shares
    scale = _inv_rms(x2)
    o_ref[0] = _rms(x2 + acc * (scale * scale), fnw_ref[...])


def _block_rest(x, q, k, v, sg, u, rf, lgf_lanes, lgb_lanes, gn_w, gn_b, conv_w, conv_b, ln_w, ln_b,
                w_out, xattn_norm_w, kt, vm, w_xq, w_xo, mlp_norm_w, w_ff1, w_ff2, final_norm_w):
    b, s, d = x.shape
    t = REST_TILE
    R = RET_WIDTH
    n_tiles = s // t
    n_steps = b * n_tiles
    hb = t // HALO
    n_hb = s // HALO

    def batch_tile(step):
        return step // n_tiles, n_tiles - 1 - step % n_tiles

    def ahead(step):
        return jnp.minimum(step + 1, n_steps - 1)

    def main_map(step):
        bi, ti = batch_tile(step)
        return bi, ti, 0

    def prev_halo_map(step):
        bi, ti = batch_tile(step)
        return bi, jnp.maximum(ti * hb - 1, 0), 0

    def next_halo_map(step):
        bi, ti = batch_tile(step)
        return bi, jnp.minimum((ti + 1) * hb, n_hb - 1), 0

    act_spec = pl.BlockSpec((1, t, R), main_map)
    halo_shape = (1, HALO, R)
    m = kt.shape[2]
    return pl.pallas_call(
        functools.partial(_rest_body, n_tiles),
        grid=(n_steps,),
        in_specs=[pl.BlockSpec((1, t, d), main_map),
                  act_spec, act_spec, act_spec, act_spec, act_spec,
                  pl.BlockSpec(halo_shape, prev_halo_map),
                  pl.BlockSpec(halo_shape, next_halo_map),
                  pl.BlockSpec((1, t, R), lambda st: main_map(ahead(st))),
                  pl.BlockSpec(halo_shape, lambda st: prev_halo_map(ahead(st))),
                  pl.BlockSpec(halo_shape, lambda st: next_halo_map(ahead(st))),
                  pl.BlockSpec((1, t // CHUNK, CHUNK, R), lambda st: main_map(st) + (0,)),
                  _const_spec((1, R)), _const_spec((1, R)),
                  _const_spec((1, R)), _const_spec((1, R)),
                  _const_spec((CONV_KERNEL, CONV_WIDTH)),
                  _const_spec((1, CONV_WIDTH)), _const_spec((1, CONV_WIDTH)), _const_spec((1, CONV_WIDTH)),
                  _const_spec((d, d)),
                  _const_spec((1, d)),
                  pl.BlockSpec((1, d, m), lambda st: (st // n_tiles, 0, 0)),
                  pl.BlockSpec((1, m, d), lambda st: (st // n_tiles, 0, 0)),
                  _const_spec((d, d)), _const_spec((d, d)),
                  _const_spec((1, d)),
                  _const_spec((d, D_FF)), _const_spec((D_FF, d)),
                  _const_spec((1, d))],
        out_specs=pl.BlockSpec((1, t, d), main_map),
        out_shape=jax.ShapeDtypeStruct((b, s, d), F32),
        scratch_shapes=[pltpu.VMEM((CHUNK, R), F32),
                        pltpu.VMEM((t, D_MODEL), BF16),
                        pltpu.VMEM((t + 2 * HALO, CONV_WIDTH), F32),
                        pltpu.VMEM((2, SUBLANES, CONV_SPAN + SHIFT_ROWS, CONV_WIDTH), F32),
                        pltpu.VMEM((t, CONV_WIDTH), BF16),
                        pltpu.VMEM((CONV_KERNEL + 3, SUBLANES, CONV_WIDTH), F32)],
        compiler_params=pltpu.CompilerParams(
            dimension_semantics=("arbitrary",), vmem_limit_bytes=VMEM_LIMIT_BYTES),
        name="block_rest",
    )(x, q, k, v, sg, u, u, u, u, u, u, rf, lgf_lanes, lgb_lanes, gn_w, gn_b, conv_w, conv_b, ln_w, ln_b,
      w_out, xattn_norm_w, kt, vm, w_xq, w_xo, mlp_norm_w, w_ff1, w_ff2, final_norm_w)


def kernel(x, mem, positions, mix_norm_w, w_in, ret_decay_f, ret_decay_b, ret_gn_w, ret_gn_b,
           conv_w, conv_b, conv_ln_w, conv_ln_b, w_out, xattn_norm_w, mem_norm_w,
           w_xq, w_xkv, w_xo, mlp_norm_w, w_ff1, w_ff2, final_norm_w):
    assert w_in.shape[0] == 1, "single-layer block"
    b, s, d = x.shape
    half = HEAD_DIM // 2
    inv_freq = ROPE_THETA ** (-jnp.arange(half, dtype=F32) / half)
    invf2 = jnp.concatenate([inv_freq, inv_freq])[:, None]
    pos3 = positions.reshape(b, 1, s)
    row = lambda a: a.reshape(1, -1)
    lgf_lanes = jnp.repeat(jax.nn.log_sigmoid(ret_decay_f[0].astype(F32)), HEAD_DIM)[None]
    lgb_lanes = jnp.repeat(jax.nn.log_sigmoid(ret_decay_b[0].astype(F32)), HEAD_DIM)[None]
    kt, vm = _mem_kv(mem, row(mem_norm_w[0]), w_xkv[0])
    (q, k, v, sg, u, rf, w_out_b, w_xq_b, w_xo_b, w_ff1_b, w_ff2_b) = _in_proj(
        x, pos3, row(mix_norm_w[0]), invf2, lgf_lanes, w_in[0],
        (w_out[0], w_xq[0], w_xo[0], w_ff1[0], w_ff2[0]))
    return _block_rest(x, q, k, v, sg, u, rf, lgf_lanes, lgb_lanes,
                       row(ret_gn_w[0]), row(ret_gn_b[0]),
                       conv_w[0].reshape(CONV_KERNEL, CONV_WIDTH), row(conv_b[0]),
                       row(conv_ln_w[0]), row(conv_ln_b[0]),
                       w_out_b, row(xattn_norm_w[0]), kt, vm, w_xq_b, w_xo_b,
                       row(mlp_norm_w[0]), w_ff1_b, w_ff2_b, row(final_norm_w))
```

```python
import functools

import jax
import jax.numpy as jnp
import numpy as np
from jax import lax
from jax.experimental import pallas as pl
from jax.experimental.pallas import tpu as pltpu

F32 = jnp.float32
BF16 = jnp.bfloat16

D_MODEL = 1024
RET_WIDTH = 512
CONV_WIDTH = 512
RET_HEADS = 4
HEAD_DIM = 128
CHUNK = 128
CONV_KERNEL = 31
CONV_PAD = CONV_KERNEL // 2
HALO = 16
XATTN_HEADS = 4
XATTN_HEAD_DIM = 256
D_FF = 4096
ROPE_THETA = 10000.0
EPS = 1e-6

IN_PROJ_TILE = 1024
REST_TILE = 512
FF_CHUNK = 512
CONV_ROWS = 32
CONV_SPAN = 128
CAST_ROWS = 128
HALF_PI_1 = 1.5703125
HALF_PI_2 = 4.837512969970703125e-4
HALF_PI_3 = 7.54978995489188216e-8
SUBLANES = 8
BF16_SUBLANES = 16
LANES = 128
SHIFT_ROWS = ((HALO - CONV_PAD + CONV_KERNEL - 1) // SUBLANES) * SUBLANES
VMEM_LIMIT_BYTES = 58 * 1024 * 1024


def _inv_rms(x):
    return lax.rsqrt(jnp.mean(x * x, axis=-1, keepdims=True) + EPS)


def _rms(x, w):
    return x * _inv_rms(x) * w


def _dot(a, b):
    return jnp.dot(a, b, preferred_element_type=F32)


def _dot_tn(a, b):
    return lax.dot_general(a, b, (((0,), (0,)), ((), ())), preferred_element_type=F32)


def _dot_nt(a, b):
    return lax.dot_general(a, b, (((1,), (1,)), ((), ())), preferred_element_type=F32)


def _block_diag(a):
    n, w2 = a.shape
    zero = jnp.zeros((n, w2 // 2), a.dtype)
    return jnp.concatenate([jnp.concatenate([a[:, :w2 // 2], zero], axis=1),
                            jnp.concatenate([zero, a[:, w2 // 2:]], axis=1)], axis=0)


def _const_spec(shape):
    nd = len(shape)
    return pl.BlockSpec(shape, lambda *_: (0,) * nd, pipeline_mode=pl.Buffered(1))


def _mem_kv_body(mem_ref, nw_ref, wkv_ref, kt_ref, v_ref):
    m = _rms(mem_ref[0], nw_ref[...]).astype(BF16)
    kv = _dot(m, wkv_ref[...].astype(BF16))
    k = kv[:, :D_MODEL] * (XATTN_HEAD_DIM ** -0.5)
    kt_ref[0] = k.T.astype(BF16)
    v_ref[0] = kv[:, D_MODEL:].astype(BF16)


def _mem_kv(mem, mem_norm_w, w_xkv):
    b, m, d = mem.shape
    return pl.pallas_call(
        _mem_kv_body,
        grid=(b,),
        in_specs=[pl.BlockSpec((1, m, d), lambda i: (i, 0, 0)),
                  _const_spec((1, d)),
                  _const_spec((d, 2 * d))],
        out_specs=[pl.BlockSpec((1, d, m), lambda i: (i, 0, 0)),
                   pl.BlockSpec((1, m, d), lambda i: (i, 0, 0))],
        out_shape=[jax.ShapeDtypeStruct((b, d, m), BF16),
                   jax.ShapeDtypeStruct((b, m, d), BF16)],
        compiler_params=pltpu.CompilerParams(
            dimension_semantics=("arbitrary",), vmem_limit_bytes=VMEM_LIMIT_BYTES),
        name="mem_kv",
    )(mem, mem_norm_w, w_xkv)


def _cos_or_sin(x, want_sin):
    k = jnp.floor(x * (2.0 / np.pi) + 0.5)
    r = ((x - k * HALF_PI_1) - k * HALF_PI_2) - k * HALF_PI_3
    r2 = r * r
    sin_r = r + r * r2 * (-1.6666654611e-1 + r2 * (8.3321608736e-3 + r2 * -1.9515295891e-4))
    cos_r = 1.0 - 0.5 * r2 + r2 * r2 * (4.166664568298827e-2
                                        + r2 * (-1.388731625493765e-3 + r2 * 2.443315711809948e-5))
    kq = k + jnp.where(want_sin, 3.0, 0.0)
    q = kq - 4.0 * jnp.floor(kq * 0.25)
    hi = jnp.floor(q * 0.5)
    lo = q - 2.0 * hi
    return (1.0 - 2.0 * hi) * (cos_r - lo * (cos_r + sin_r))


def _in_proj_body(x_ref, pos_ref, nw_ref, invf_ref, lgf_ref, w_ref,
                  wout_ref, wxq_ref, wxo_ref, w1_ref, w2_ref,
                  q_ref, k_ref, v_ref, sg_ref, u_ref, rf_ref,
                  wout_o, wxq_o, wxo_o, w1_o, w2_o, state_ref, wbf_ref):
    t = x_ref.shape[1]
    R = RET_WIDTH

    @pl.when((pl.program_id(0) == 0) & (pl.program_id(1) == 0))
    def _():
        for r0 in range(0, D_MODEL, CAST_ROWS):
            wbf_ref[r0:r0 + CAST_ROWS, :] = w_ref[r0:r0 + CAST_ROWS, :].astype(BF16)

    @pl.when(pl.program_id(1) == 0)
    def _():
        state_ref[...] = jnp.zeros_like(state_ref)

    for src, dst in ((wout_ref, wout_o), (wxq_ref, wxq_o), (wxo_ref, wxo_o), (w1_ref, w1_o), (w2_ref, w2_o)):
        dst[...] = src[...].astype(BF16)

    h = _rms(x_ref[0], nw_ref[...]).astype(BF16)

    ang_t = invf_ref[...] * pos_ref[0].astype(F32)
    cs = _cos_or_sin(ang_t, lax.broadcasted_iota(jnp.int32, (HEAD_DIM, 1), 0) >= HEAD_DIM // 2).T
    upper = lax.broadcasted_iota(jnp.int32, (1, HEAD_DIM), 1) >= HEAD_DIM // 2
    sc = pltpu.roll(cs, HEAD_DIM // 2, 1)
    cos2 = jnp.where(upper, sc, cs)
    sin2 = jnp.where(upper, cs, -sc)

    def rope(p):
        outs = []
        for hd in range(RET_HEADS):
            ph = p[:, hd * HEAD_DIM:(hd + 1) * HEAD_DIM]
            outs.append(ph * cos2 + pltpu.roll(ph, HEAD_DIM // 2, 1) * sin2)
        return jnp.concatenate(outs, axis=1)

    q = rope(_dot(h, wbf_ref[:, 0:R]))
    q_ref[0] = q.astype(BF16)
    k = rope(_dot(h, wbf_ref[:, R:2 * R])) * (HEAD_DIM ** -0.5)
    k_ref[0] = k.astype(BF16)
    v = _dot(h, wbf_ref[:, 2 * R:3 * R]).astype(BF16)
    v_ref[0] = v
    g = _dot(h, wbf_ref[:, 3 * R:4 * R])
    sg_ref[0] = (g * jax.nn.sigmoid(g)).astype(BF16)
    a = _dot(h, wbf_ref[:, 4 * R:5 * R])
    bb = _dot(h, wbf_ref[:, 5 * R:6 * R])
    u_ref[0] = (a * jax.nn.sigmoid(bb)).astype(BF16)

    lgf = lgf_ref[...]
    row = lax.broadcasted_iota(jnp.int32, (CHUNK, R), 0).astype(F32)
    zeta = jnp.exp(lgf * (CHUNK - 1.0 - row))
    chunk_decay = jnp.exp(lgf * float(CHUNK))
    state = state_ref[...]
    for c in range(t // CHUNK):
        rf_ref[0, c] = state.astype(BF16)
        rows = slice(c * CHUNK, (c + 1) * CHUNK)
        kz = (k[rows] * zeta).astype(BF16)
        vc = v[rows]
        upd = jnp.concatenate(
            [_dot_tn(kz[:, hd * HEAD_DIM:(hd + 1) * HEAD_DIM],
                     vc[:, hd * HEAD_DIM:(hd + 1) * HEAD_DIM]) for hd in range(RET_HEADS)], axis=1)
        state = chunk_decay * state + upd
    state_ref[...] = state


def _in_proj(x, pos3, mix_norm_w, invf2, lgf_lanes, w_in, later_weights):
    b, s, d = x.shape
    t = IN_PROJ_TILE
    R = RET_WIDTH
    n_j = s // t
    n_steps = b * n_j
    tok = lambda i, j: (i, j, 0)
    act_spec = pl.BlockSpec((1, t, R), tok)
    act_shape = jax.ShapeDtypeStruct((b, s, R), BF16)
    cast_specs = []
    for w in later_weights:
        rows, cols = w.shape
        assert rows % (n_steps * BF16_SUBLANES) == 0
        cast_specs.append(pl.BlockSpec((rows // n_steps, cols), lambda i, j: (i * n_j + j, 0)))
    return pl.pallas_call(
        _in_proj_body,
        grid=(b, n_j),
        in_specs=[pl.BlockSpec((1, t, d), tok),
                  pl.BlockSpec((1, 1, t), lambda i, j: (i, 0, j)),
                  _const_spec((1, d)),
                  _const_spec((HEAD_DIM, 1)),
                  _const_spec((1, R)),
                  _const_spec((d, 6 * R))] + cast_specs,
        out_specs=([act_spec] * 5 + [pl.BlockSpec((1, t // CHUNK, CHUNK, R), lambda i, j: (i, j, 0, 0))]
                   + cast_specs),
        out_shape=([act_shape] * 5 + [jax.ShapeDtypeStruct((b, s // CHUNK, CHUNK, R), BF16)]
                   + [jax.ShapeDtypeStruct(w.shape, BF16) for w in later_weights]),
        scratch_shapes=[pltpu.VMEM((CHUNK, R), F32),
                        pltpu.VMEM((d, 6 * R), BF16)],
        compiler_params=pltpu.CompilerParams(
            dimension_semantics=("arbitrary", "arbitrary"), vmem_limit_bytes=VMEM_LIMIT_BYTES),
        name="in_proj",
    )(x, pos3, mix_norm_w, invf2, lgf_lanes, w_in, *later_weights)


def _conv_pieces(u_ref, up_ref, un_ref, has_prev, has_next,
                 cpar_ref, uext_ref, ush_ref, yconv_ref):
    t = u_ref.shape[1]
    span = CONV_SPAN + SHIFT_ROWS
    groups = (CONV_ROWS // SUBLANES, SUBLANES, CONV_WIDTH)
    block_vregs = CONV_ROWS * CONV_WIDTH // (SUBLANES * LANES)
    span_vregs = span * CONV_WIDTH // (SUBLANES * LANES)
    accs = {}

    def fill(gate):
        del gate
        zero_halo = jnp.zeros((HALO, CONV_WIDTH), F32)
        uext_ref[0:HALO, :] = jnp.where(has_prev, up_ref[0].astype(F32), zero_halo)
        uext_ref[HALO:HALO + t, :] = u_ref[0].astype(F32)
        uext_ref[HALO + t:HALO + t + HALO, :] = jnp.where(has_next, un_ref[0].astype(F32), zero_halo)

    def shift(hb, sft, gate):
        ush_ref[(hb // CONV_SPAN) % 2, sft, 0:span, :] = _gated(uext_ref[hb + sft:hb + sft + span, :], gate)

    def tap_step(hb, base, tap, gate):
        grp, sft = divmod(HALO - CONV_PAD + tap, SUBLANES)
        lo = base + grp * SUBLANES
        if sft == 0:
            src = uext_ref[hb + lo:hb + lo + CONV_ROWS, :]
        else:
            src = ush_ref[(hb // CONV_SPAN) % 2, sft, lo:lo + CONV_ROWS, :]
        acc = accs.pop((hb, base)) if tap else jnp.broadcast_to(cpar_ref[CONV_KERNEL][None], groups)
        weight = cpar_ref[tap]
        weight = _gated(weight, gate)
        acc = acc + weight[None] * src.reshape(groups)
        if tap < CONV_KERNEL - 1:
            accs[(hb, base)] = acc
            return
        mu = jnp.mean(acc, axis=-1, keepdims=True)
        yc = acc - mu
        var = jnp.mean(yc * yc, axis=-1, keepdims=True)
        yl = yc * lax.rsqrt(var + EPS) * cpar_ref[CONV_KERNEL + 1][None] + cpar_ref[CONV_KERNEL + 2][None]
        out = (yl * jax.nn.sigmoid(yl)).reshape(CONV_ROWS, CONV_WIDTH)
        yconv_ref[hb + base:hb + base + CONV_ROWS, :] = out.astype(BF16)

    def copies(hb):
        return [(3 * span_vregs, functools.partial(shift, hb, sft)) for sft in range(1, SUBLANES)]

    def taps(hb):
        return [(2 * block_vregs + (12 * block_vregs if tap == CONV_KERNEL - 1 else 0),
                 functools.partial(tap_step, hb, base, tap))
                for base in range(0, CONV_SPAN, CONV_ROWS) for tap in range(CONV_KERNEL)]

    work = [(2 * (t + 2 * HALO) * CONV_WIDTH // (SUBLANES * LANES), fill)] + copies(0)
    for hb in range(0, t, CONV_SPAN):
        ahead = copies(hb + CONV_SPAN) if hb + CONV_SPAN < t else []
        steps = taps(hb)
        every = len(steps) // (len(ahead) + 1)
        for i, step in enumerate(steps):
            work.append(step)
            if ahead and (i + 1) % every == 0:
                work.append(ahead.pop(0))
        assert not ahead
    return work


def _gated(x, gate):
    if gate is None:
        return x
    rows, cols = x.shape
    return x + jnp.tile(gate, (rows // SUBLANES, cols // LANES))


def _gates_from(y, n):
    rows, cols = y.shape
    tiles = [(r, c) for c in range(0, cols, LANES) for r in range(0, rows, SUBLANES)]
    picks = [tiles[(i * len(tiles)) // n] for i in range(n)]
    return [pltpu.bitcast((pltpu.bitcast(y[r:r + SUBLANES, c:c + LANES], jnp.uint32) >> 16) >> 16, F32)
            for r, c in picks]


def _rest_body(n_tiles,
               x_ref, q_ref, k_ref, v_ref, sg_ref, u_ref, up_ref, un_ref, u1_ref, up1_ref, un1_ref, rf_ref,
               lgf_ref, lgb_ref, gnw_ref, gnb_ref, cw_ref, cb_ref, lnw_ref, lnb_ref,
               wout_ref, xnw_ref, kt_ref, vm_ref, wxq_ref, wxo_ref,
               mnw_ref, w1_ref, w2_ref, fnw_ref,
               o_ref, rb_ref, mix_ref, uext_ref, ush_ref, yconv_ref, cpar_ref):
    t = x_ref.shape[1]
    R = RET_WIDTH
    step = pl.program_id(0)
    last_step = pl.num_programs(0) - 1
    tile = n_tiles - 1 - step % n_tiles
    step1 = jnp.minimum(step + 1, last_step)
    tile1 = n_tiles - 1 - step1 % n_tiles
    conv_consts = (cpar_ref, uext_ref, ush_ref, yconv_ref)

    @pl.when(step % n_tiles == 0)
    def _():
        rb_ref[...] = jnp.zeros_like(rb_ref)

    @pl.when(step == 0)
    def _():
        for tap in range(CONV_KERNEL):
            cpar_ref[tap] = jnp.broadcast_to(cw_ref[tap:tap + 1, :], (SUBLANES, CONV_WIDTH))
        for i, par in enumerate((cb_ref, lnw_ref, lnb_ref)):
            cpar_ref[CONV_KERNEL + i] = jnp.broadcast_to(par[...], (SUBLANES, CONV_WIDTH))
        for _, item in _conv_pieces(u_ref, up_ref, un_ref, tile > 0, tile < n_tiles - 1, *conv_consts):
            item(None)

    mix_ref[:, R:R + CONV_WIDTH] = yconv_ref[...]

    lgf = lgf_ref[...]
    lgb = lgb_ref[...]
    row = lax.broadcasted_iota(jnp.int32, (CHUNK, R), 0).astype(F32)
    xi_f = jnp.exp(lgf * (row + 1.0))
    xi_b = jnp.exp(lgb * (float(CHUNK) - row))
    zeta_b = jnp.exp(lgb * row)
    decay_b = jnp.exp(lgb * float(CHUNK))
    ii = lax.broadcasted_iota(jnp.int32, (CHUNK, CHUNK), 0)
    jj = lax.broadcasted_iota(jnp.int32, (CHUNK, CHUNK), 1)
    diff = (ii - jj).astype(F32)
    dmask = jnp.concatenate(
        [jnp.where(diff >= 0.0,
                   jnp.exp(lgf[:, hd * HEAD_DIM:(hd + 1) * HEAD_DIM] * jnp.maximum(diff, 0.0)),
                   jnp.exp(lgb[:, hd * HEAD_DIM:(hd + 1) * HEAD_DIM] * jnp.maximum(-diff, 0.0)))
         for hd in range(RET_HEADS)], axis=1)

    gnw = gnw_ref[...]
    gnb = gnb_ref[...]
    rb = rb_ref[...]
    for c in reversed(range(t // CHUNK)):
        rows = slice(c * CHUNK, (c + 1) * CHUNK)
        qc = q_ref[0, rows, :]
        kc = k_ref[0, rows, :]
        vc = v_ref[0, rows, :]
        sgc = sg_ref[0, rows, :].astype(F32)
        rfc = rf_ref[0, c]
        rbb = rb.astype(BF16)
        qf = qc.astype(F32)
        qxf = (qf * xi_f).astype(BF16)
        qxb = (qf * xi_b).astype(BF16)
        kzb = (kc.astype(F32) * zeta_b).astype(BF16)
        ys = []
        upds = []
        for ps in (slice(0, 2 * HEAD_DIM), slice(2 * HEAD_DIM, 4 * HEAD_DIM)):
            p = (_dot_nt(qc[:, ps], _block_diag(kc[:, ps])) * dmask[:, ps]).astype(BF16)
            lhs = jnp.concatenate([p, qxf[:, ps], qxb[:, ps]], axis=1)
            rhs = jnp.concatenate([_block_diag(vc[:, ps]), _block_diag(rfc[:, ps]), _block_diag(rbb[:, ps])],
                                  axis=0)
            ys.append(_dot(lhs, rhs))
            kv = _dot_tn(kzb[:, ps], vc[:, ps])
            upds += [kv[:HEAD_DIM, :HEAD_DIM], kv[HEAD_DIM:, HEAD_DIM:]]
        yn = []
        for hd, y in enumerate(ys):
            for half in (y[:, :HEAD_DIM], y[:, HEAD_DIM:]):
                mu = jnp.mean(half, axis=-1, keepdims=True)
                yc = half - mu
                var = jnp.mean(yc * yc, axis=-1, keepdims=True)
                yn.append(yc * lax.rsqrt(var + EPS))
        yn = jnp.concatenate(yn, axis=1) * gnw + gnb
        mix_ref[rows, 0:R] = (sgc * yn).astype(BF16)
        rb = decay_b * rb + jnp.concatenate(upds, axis=1)
    rb_ref[...] = rb

    work = _conv_pieces(u1_ref, up1_ref, un1_ref, tile1 > 0, tile1 < n_tiles - 1, *conv_consts)
    total_cost = sum(cost for cost, _ in work)
    n_chunks = D_FF // FF_CHUNK
    n_shares = 3 + 2 * n_chunks - 1
    progress = {"share": 0, "item": 0, "cost": 0}

    def conv_share(result):
        progress["share"] += 1
        items = []
        while progress["item"] < len(work) and progress["cost"] * n_shares < total_cost * progress["share"]:
            cost, item = work[progress["item"]]
            items.append(item)
            progress["item"] += 1
            progress["cost"] += cost
        for item, gate in zip(items, _gates_from(result, len(items))):
            item(gate)
        return result

    x1 = x_ref[0] + conv_share(_dot(mix_ref[...], wout_ref[...]))

    qx = _dot((x1 * xnw_ref[...]).astype(BF16), wxq_ref[...])
    qx = (conv_share(qx) * _inv_rms(x1)).astype(BF16)
    outs = []
    for hd in range(XATTN_HEADS):
        hs = slice(hd * XATTN_HEAD_DIM, (hd + 1) * XATTN_HEAD_DIM)
        s = _dot(qx[:, hs], kt_ref[0, hs, :])
        e = jnp.exp(s - jnp.max(s, axis=-1, keepdims=True))
        p = e * (1.0 / jnp.sum(e, axis=-1, keepdims=True))
        outs.append(_dot(p.astype(BF16), vm_ref[0, :, hs]).astype(BF16))
    x2 = x1 + conv_share(_dot(jnp.concatenate(outs, axis=1), wxo_ref[...]))

    hm = (x2 * mnw_ref[...]).astype(BF16)
    acc = jnp.zeros((t, D_MODEL), F32)
    for c in range(n_chunks):
        cs = slice(c * FF_CHUNK, (c + 1) * FF_CHUNK)
        hc = conv_share(jnp.maximum(_dot(hm, w1_ref[:, cs]), 0.0))
        part = _dot((hc * hc).astype(BF16), w2_ref[cs, :])
        acc = acc + (conv_share(part) if c < n_chunks - 1 else part)
    assert progress["item"] == len(work) and progress["share"] == n_shares
    scale = _inv_rms(x2)
    o_ref[0] = _rms(x2 + acc * (scale * scale), fnw_ref[...])


def _block_rest(x, q, k, v, sg, u, rf, lgf_lanes, lgb_lanes, gn_w, gn_b, conv_w, conv_b, ln_w, ln_b,
                w_out, xattn_norm_w, kt, vm, w_xq, w_xo, mlp_norm_w, w_ff1, w_ff2, final_norm_w):
    b, s, d = x.shape
    t = REST_TILE
    R = RET_WIDTH
    n_tiles = s // t
    n_steps = b * n_tiles
    hb = t // HALO
    n_hb = s // HALO

    def batch_tile(step):
        return step // n_tiles, n_tiles - 1 - step % n_tiles

    def ahead(step):
        return jnp.minimum(step + 1, n_steps - 1)

    def main_map(step):
        bi, ti = batch_tile(step)
        return bi, ti, 0

    def prev_halo_map(step):
        bi, ti = batch_tile(step)
        return bi, jnp.maximum(ti * hb - 1, 0), 0

    def next_halo_map(step):
        bi, ti = batch_tile(step)
        return bi, jnp.minimum((ti + 1) * hb, n_hb - 1), 0

    act_spec = pl.BlockSpec((1, t, R), main_map)
    halo_shape = (1, HALO, R)
    m = kt.shape[2]
    return pl.pallas_call(
        functools.partial(_rest_body, n_tiles),
        grid=(n_steps,),
        in_specs=[pl.BlockSpec((1, t, d), main_map),
                  act_spec, act_spec, act_spec, act_spec, act_spec,
                  pl.BlockSpec(halo_shape, prev_halo_map),
                  pl.BlockSpec(halo_shape, next_halo_map),
                  pl.BlockSpec((1, t, R), lambda st: main_map(ahead(st))),
                  pl.BlockSpec(halo_shape, lambda st: prev_halo_map(ahead(st))),
                  pl.BlockSpec(halo_shape, lambda st: next_halo_map(ahead(st))),
                  pl.BlockSpec((1, t // CHUNK, CHUNK, R), lambda st: main_map(st) + (0,)),
                  _const_spec((1, R)), _const_spec((1, R)),
                  _const_spec((1, R)), _const_spec((1, R)),
                  _const_spec((CONV_KERNEL, CONV_WIDTH)),
                  _const_spec((1, CONV_WIDTH)), _const_spec((1, CONV_WIDTH)), _const_spec((1, CONV_WIDTH)),
                  _const_spec((d, d)),
                  _const_spec((1, d)),
                  pl.BlockSpec((1, d, m), lambda st: (st // n_tiles, 0, 0)),
                  pl.BlockSpec((1, m, d), lambda st: (st // n_tiles, 0, 0)),
                  _const_spec((d, d)), _const_spec((d, d)),
                  _const_spec((1, d)),
                  _const_spec((d, D_FF)), _const_spec((D_FF, d)),
                  _const_spec((1, d))],
        out_specs=pl.BlockSpec((1, t, d), main_map),
        out_shape=jax.ShapeDtypeStruct((b, s, d), F32),
        scratch_shapes=[pltpu.VMEM((CHUNK, R), F32),
                        pltpu.VMEM((t, D_MODEL), BF16),
                        pltpu.VMEM((t + 2 * HALO, CONV_WIDTH), F32),
                        pltpu.VMEM((2, SUBLANES, CONV_SPAN + SHIFT_ROWS, CONV_WIDTH), F32),
                        pltpu.VMEM((t, CONV_WIDTH), BF16),
                        pltpu.VMEM((CONV_KERNEL + 3, SUBLANES, CONV_WIDTH), F32)],
        compiler_params=pltpu.CompilerParams(
            dimension_semantics=("arbitrary",), vmem_limit_bytes=VMEM_LIMIT_BYTES),
        name="block_rest",
    )(x, q, k, v, sg, u, u, u, u, u, u, rf, lgf_lanes, lgb_lanes, gn_w, gn_b, conv_w, conv_b, ln_w, ln_b,
      w_out, xattn_norm_w, kt, vm, w_xq, w_xo, mlp_norm_w, w_ff1, w_ff2, final_norm_w)


def kernel(x, mem, positions, mix_norm_w, w_in, ret_decay_f, ret_decay_b, ret_gn_w, ret_gn_b,
           conv_w, conv_b, conv_ln_w, conv_ln_b, w_out, xattn_norm_w, mem_norm_w,
           w_xq, w_xkv, w_xo, mlp_norm_w, w_ff1, w_ff2, final_norm_w):
    assert w_in.shape[0] == 1, "single-layer block"
    b, s, d = x.shape
    half = HEAD_DIM // 2
    inv_freq = ROPE_THETA ** (-jnp.arange(half, dtype=F32) / half)
    invf2 = jnp.concatenate([inv_freq, inv_freq])[:, None]
    pos3 = positions.reshape(b, 1, s)
    row = lambda a: a.reshape(1, -1)
    lgf_lanes = jnp.repeat(jax.nn.log_sigmoid(ret_decay_f[0].astype(F32)), HEAD_DIM)[None]
    lgb_lanes = jnp.repeat(jax.nn.log_sigmoid(ret_decay_b[0].astype(F32)), HEAD_DIM)[None]
    kt, vm = _mem_kv(mem, row(mem_norm_w[0]), w_xkv[0])
    (q, k, v, sg, u, rf, w_out_b, w_xq_b, w_xo_b, w_ff1_b, w_ff2_b) = _in_proj(
        x, pos3, row(mix_norm_w[0]), invf2, lgf_lanes, w_in[0],
        (w_out[0], w_xq[0], w_xo[0], w_ff1[0], w_ff2[0]))
    return _block_rest(x, q, k, v, sg, u, rf, lgf_lanes, lgb_lanes,
                       row(ret_gn_w[0]), row(ret_gn_b[0]),
                       conv_w[0].reshape(CONV_KERNEL, CONV_WIDTH), row(conv_b[0]),
                       row(conv_ln_w[0]), row(conv_ln_b[0]),
                       w_out_b, row(xattn_norm_w[0]), kt, vm, w_xq_b, w_xo_b,
                       row(mlp_norm_w[0]), w_ff1_b, w_ff2_b, row(final_norm_w))
```

```python
import functools

import jax
import jax.numpy as jnp
import numpy as np
from jax import lax
from jax.experimental import pallas as pl
from jax.experimental.pallas import tpu as pltpu

F32 = jnp.float32
BF16 = jnp.bfloat16

D_MODEL = 1024
RET_WIDTH = 512
CONV_WIDTH = 512
RET_HEADS = 4
HEAD_DIM = 128
CHUNK = 128
CONV_KERNEL = 31
CONV_PAD = CONV_KERNEL // 2
HALO = 16
XATTN_HEADS = 4
XATTN_HEAD_DIM = 256
D_FF = 4096
ROPE_THETA = 10000.0
EPS = 1e-6

IN_PROJ_TILE = 512
REST_TILE = 512
FF_CHUNK = 512
CONV_ROWS = 32
CONV_SPAN = 128
CAST_ROWS = 128
HALF_PI_1 = 1.5703125
HALF_PI_2 = 4.837512969970703125e-4
HALF_PI_3 = 7.54978995489188216e-8
SUBLANES = 8
BF16_SUBLANES = 16
LANES = 128
SHIFT_ROWS = ((HALO - CONV_PAD + CONV_KERNEL - 1) // SUBLANES) * SUBLANES
VMEM_LIMIT_BYTES = 58 * 1024 * 1024


def _inv_rms(x):
    return lax.rsqrt(jnp.mean(x * x, axis=-1, keepdims=True) + EPS)


def _rms(x, w):
    return x * _inv_rms(x) * w


def _dot(a, b):
    return jnp.dot(a, b, preferred_element_type=F32)


def _dot_tn(a, b):
    return lax.dot_general(a, b, (((0,), (0,)), ((), ())), preferred_element_type=F32)


def _dot_nt(a, b):
    return lax.dot_general(a, b, (((1,), (1,)), ((), ())), preferred_element_type=F32)


def _block_diag(a):
    n, w2 = a.shape
    zero = jnp.zeros((n, w2 // 2), a.dtype)
    return jnp.concatenate([jnp.concatenate([a[:, :w2 // 2], zero], axis=1),
                            jnp.concatenate([zero, a[:, w2 // 2:]], axis=1)], axis=0)


def _const_spec(shape):
    nd = len(shape)
    return pl.BlockSpec(shape, lambda *_: (0,) * nd, pipeline_mode=pl.Buffered(1))


def _mem_kv_body(mem_ref, nw_ref, wkv_ref, kt_ref, v_ref):
    m = _rms(mem_ref[0], nw_ref[...]).astype(BF16)
    kv = _dot(m, wkv_ref[...].astype(BF16))
    k = kv[:, :D_MODEL] * (XATTN_HEAD_DIM ** -0.5)
    kt_ref[0] = k.T.astype(BF16)
    v_ref[0] = kv[:, D_MODEL:].astype(BF16)


def _mem_kv(mem, mem_norm_w, w_xkv):
    b, m, d = mem.shape
    return pl.pallas_call(
        _mem_kv_body,
        grid=(b,),
        in_specs=[pl.BlockSpec((1, m, d), lambda i: (i, 0, 0)),
                  _const_spec((1, d)),
                  _const_spec((d, 2 * d))],
        out_specs=[pl.BlockSpec((1, d, m), lambda i: (i, 0, 0)),
                   pl.BlockSpec((1, m, d), lambda i: (i, 0, 0))],
        out_shape=[jax.ShapeDtypeStruct((b, d, m), BF16),
                   jax.ShapeDtypeStruct((b, m, d), BF16)],
        compiler_params=pltpu.CompilerParams(
            dimension_semantics=("arbitrary",), vmem_limit_bytes=VMEM_LIMIT_BYTES),
        name="mem_kv",
    )(mem, mem_norm_w, w_xkv)


def _cos_or_sin(x, want_sin):
    k = jnp.floor(x * (2.0 / np.pi) + 0.5)
    r = ((x - k * HALF_PI_1) - k * HALF_PI_2) - k * HALF_PI_3
    r2 = r * r
    sin_r = r + r * r2 * (-1.6666654611e-1 + r2 * (8.3321608736e-3 + r2 * -1.9515295891e-4))
    cos_r = 1.0 - 0.5 * r2 + r2 * r2 * (4.166664568298827e-2
                                        + r2 * (-1.388731625493765e-3 + r2 * 2.443315711809948e-5))
    kq = k + jnp.where(want_sin, 3.0, 0.0)
    q = kq - 4.0 * jnp.floor(kq * 0.25)
    hi = jnp.floor(q * 0.5)
    lo = q - 2.0 * hi
    return (1.0 - 2.0 * hi) * (cos_r - lo * (cos_r + sin_r))


def _in_proj_body(x_ref, pos_ref, nw_ref, invf_ref, lgf_ref, w_ref,
                  wout_ref, wxq_ref, wxo_ref, w1_ref, w2_ref,
                  q_ref, k_ref, v_ref, sg_ref, u_ref, rf_ref,
                  wout_o, wxq_o, wxo_o, w1_o, w2_o, state_ref, wbf_ref):
    t = x_ref.shape[1]
    R = RET_WIDTH

    @pl.when((pl.program_id(0) == 0) & (pl.program_id(1) == 0))
    def _():
        for r0 in range(0, D_MODEL, CAST_ROWS):
            wbf_ref[r0:r0 + CAST_ROWS, :] = w_ref[r0:r0 + CAST_ROWS, :].astype(BF16)

    @pl.when(pl.program_id(1) == 0)
    def _():
        state_ref[...] = jnp.zeros_like(state_ref)

    for src, dst in ((wout_ref, wout_o), (wxq_ref, wxq_o), (wxo_ref, wxo_o), (w1_ref, w1_o), (w2_ref, w2_o)):
        dst[...] = src[...].astype(BF16)

    h = _rms(x_ref[0], nw_ref[...]).astype(BF16)

    ang_t = invf_ref[...] * pos_ref[0].astype(F32)
    cs = _cos_or_sin(ang_t, lax.broadcasted_iota(jnp.int32, (HEAD_DIM, 1), 0) >= HEAD_DIM // 2).T
    upper = lax.broadcasted_iota(jnp.int32, (1, HEAD_DIM), 1) >= HEAD_DIM // 2
    sc = pltpu.roll(cs, HEAD_DIM // 2, 1)
    cos2 = jnp.where(upper, sc, cs)
    sin2 = jnp.where(upper, cs, -sc)

    def rope(p):
        outs = []
        for hd in range(RET_HEADS):
            ph = p[:, hd * HEAD_DIM:(hd + 1) * HEAD_DIM]
            outs.append(ph * cos2 + pltpu.roll(ph, HEAD_DIM // 2, 1) * sin2)
        return jnp.concatenate(outs, axis=1)

    q = rope(_dot(h, wbf_ref[:, 0:R]))
    q_ref[0] = q.astype(BF16)
    k = rope(_dot(h, wbf_ref[:, R:2 * R])) * (HEAD_DIM ** -0.5)
    k_ref[0] = k.astype(BF16)
    v = _dot(h, wbf_ref[:, 2 * R:3 * R]).astype(BF16)
    v_ref[0] = v
    g = _dot(h, wbf_ref[:, 3 * R:4 * R])
    sg_ref[0] = (g * jax.nn.sigmoid(g)).astype(BF16)
    a = _dot(h, wbf_ref[:, 4 * R:5 * R])
    bb = _dot(h, wbf_ref[:, 5 * R:6 * R])
    u_ref[0] = (a * jax.nn.sigmoid(bb)).astype(BF16)

    lgf = lgf_ref[...]
    row = lax.broadcasted_iota(jnp.int32, (CHUNK, R), 0).astype(F32)
    zeta = jnp.exp(lgf * (CHUNK - 1.0 - row))
    chunk_decay = jnp.exp(lgf * float(CHUNK))
    state = state_ref[...]
    for c in range(t // CHUNK):
        rf_ref[0, c] = state.astype(BF16)
        rows = slice(c * CHUNK, (c + 1) * CHUNK)
        kz = (k[rows] * zeta).astype(BF16)
        vc = v[rows]
        upd = jnp.concatenate(
            [_dot_tn(kz[:, hd * HEAD_DIM:(hd + 1) * HEAD_DIM],
                     vc[:, hd * HEAD_DIM:(hd + 1) * HEAD_DIM]) for hd in range(RET_HEADS)], axis=1)
        state = chunk_decay * state + upd
    state_ref[...] = state


def _in_proj(x, pos3, mix_norm_w, invf2, lgf_lanes, w_in, later_weights):
    b, s, d = x.shape
    t = IN_PROJ_TILE
    R = RET_WIDTH
    n_j = s // t
    n_steps = b * n_j
    tok = lambda i, j: (i, j, 0)
    act_spec = pl.BlockSpec((1, t, R), tok)
    act_shape = jax.ShapeDtypeStruct((b, s, R), BF16)
    cast_specs = []
    for w in later_weights:
        rows, cols = w.shape
        assert rows % (n_steps * BF16_SUBLANES) == 0
        cast_specs.append(pl.BlockSpec((rows // n_steps, cols), lambda i, j: (i * n_j + j, 0)))
    return pl.pallas_call(
        _in_proj_body,
        grid=(b, n_j),
        in_specs=[pl.BlockSpec((1, t, d), tok),
                  pl.BlockSpec((1, 1, t), lambda i, j: (i, 0, j)),
                  _const_spec((1, d)),
                  _const_spec((HEAD_DIM, 1)),
                  _const_spec((1, R)),
                  _const_spec((d, 6 * R))] + cast_specs,
        out_specs=([act_spec] * 5 + [pl.BlockSpec((1, t // CHUNK, CHUNK, R), lambda i, j: (i, j, 0, 0))]
                   + cast_specs),
        out_shape=([act_shape] * 5 + [jax.ShapeDtypeStruct((b, s // CHUNK, CHUNK, R), BF16)]
                   + [jax.ShapeDtypeStruct(w.shape, BF16) for w in later_weights]),
        scratch_shapes=[pltpu.VMEM((CHUNK, R), F32),
                        pltpu.VMEM((d, 6 * R), BF16)],
        compiler_params=pltpu.CompilerParams(
            dimension_semantics=("arbitrary", "arbitrary"), vmem_limit_bytes=VMEM_LIMIT_BYTES),
        name="in_proj",
    )(x, pos3, mix_norm_w, invf2, lgf_lanes, w_in, *later_weights)


def _conv_pieces(u_ref, up_ref, un_ref, has_prev, has_next,
                 cpar_ref, uext_ref, ush_ref, yconv_ref):
    t = u_ref.shape[1]
    span = CONV_SPAN + SHIFT_ROWS
    groups = (CONV_ROWS // SUBLANES, SUBLANES, CONV_WIDTH)
    block_vregs = CONV_ROWS * CONV_WIDTH // (SUBLANES * LANES)
    span_vregs = span * CONV_WIDTH // (SUBLANES * LANES)
    accs = {}

    def fill(gate):
        del gate
        zero_halo = jnp.zeros((HALO, CONV_WIDTH), F32)
        uext_ref[0:HALO, :] = jnp.where(has_prev, up_ref[0].astype(F32), zero_halo)
        uext_ref[HALO:HALO + t, :] = u_ref[0].astype(F32)
        uext_ref[HALO + t:HALO + t + HALO, :] = jnp.where(has_next, un_ref[0].astype(F32), zero_halo)

    def shift(hb, sft, gate):
        ush_ref[(hb // CONV_SPAN) % 2, sft, 0:span, :] = _gated(uext_ref[hb + sft:hb + sft + span, :], gate)

    def tap_step(hb, base, tap, gate):
        grp, sft = divmod(HALO - CONV_PAD + tap, SUBLANES)
        lo = base + grp * SUBLANES
        if sft == 0:
            src = uext_ref[hb + lo:hb + lo + CONV_ROWS, :]
        else:
            src = ush_ref[(hb // CONV_SPAN) % 2, sft, lo:lo + CONV_ROWS, :]
        acc = accs.pop((hb, base)) if tap else jnp.broadcast_to(cpar_ref[CONV_KERNEL][None], groups)
        weight = cpar_ref[tap]
        weight = _gated(weight, gate)
        acc = acc + weight[None] * src.reshape(groups)
        if tap < CONV_KERNEL - 1:
            accs[(hb, base)] = acc
            return
        mu = jnp.mean(acc, axis=-1, keepdims=True)
        yc = acc - mu
        var = jnp.mean(yc * yc, axis=-1, keepdims=True)
        yl = yc * lax.rsqrt(var + EPS) * cpar_ref[CONV_KERNEL + 1][None] + cpar_ref[CONV_KERNEL + 2][None]
        out = (yl * jax.nn.sigmoid(yl)).reshape(CONV_ROWS, CONV_WIDTH)
        yconv_ref[hb + base:hb + base + CONV_ROWS, :] = out.astype(BF16)

    def copies(hb):
        return [(3 * span_vregs, functools.partial(shift, hb, sft)) for sft in range(1, SUBLANES)]

    def taps(hb):
        return [(2 * block_vregs + (12 * block_vregs if tap == CONV_KERNEL - 1 else 0),
                 functools.partial(tap_step, hb, base, tap))
                for base in range(0, CONV_SPAN, CONV_ROWS) for tap in range(CONV_KERNEL)]

    work = [(2 * (t + 2 * HALO) * CONV_WIDTH // (SUBLANES * LANES), fill)] + copies(0)
    for hb in range(0, t, CONV_SPAN):
        ahead = copies(hb + CONV_SPAN) if hb + CONV_SPAN < t else []
        steps = taps(hb)
        every = len(steps) // (len(ahead) + 1)
        for i, step in enumerate(steps):
            work.append(step)
            if ahead and (i + 1) % every == 0:
                work.append(ahead.pop(0))
        assert not ahead
    return work


def _gated(x, gate):
    if gate is None:
        return x
    rows, cols = x.shape
    return x + jnp.tile(gate, (rows // SUBLANES, cols // LANES))


def _gates_from(y, n):
    rows, cols = y.shape
    tiles = [(r, c) for c in range(0, cols, LANES) for r in range(0, rows, SUBLANES)]
    picks = [tiles[(i * len(tiles)) // n] for i in range(n)]
    return [pltpu.bitcast((pltpu.bitcast(y[r:r + SUBLANES, c:c + LANES], jnp.uint32) >> 16) >> 16, F32)
            for r, c in picks]


def _rest_body(n_tiles,
               x_ref, q_ref, k_ref, v_ref, sg_ref, u_ref, up_ref, un_ref, u1_ref, up1_ref, un1_ref, rf_ref,
               lgf_ref, lgb_ref, gnw_ref, gnb_ref, cw_ref, cb_ref, lnw_ref, lnb_ref,
               wout_ref, xnw_ref, kt_ref, vm_ref, wxq_ref, wxo_ref,
               mnw_ref, w1_ref, w2_ref, fnw_ref,
               o_ref, rb_ref, mix_ref, uext_ref, ush_ref, yconv_ref, cpar_ref):
    t = x_ref.shape[1]
    R = RET_WIDTH
    step = pl.program_id(0)
    last_step = pl.num_programs(0) - 1
    tile = n_tiles - 1 - step % n_tiles
    step1 = jnp.minimum(step + 1, last_step)
    tile1 = n_tiles - 1 - step1 % n_tiles
    conv_consts = (cpar_ref, uext_ref, ush_ref, yconv_ref)

    @pl.when(step % n_tiles == 0)
    def _():
        rb_ref[...] = jnp.zeros_like(rb_ref)

    @pl.when(step == 0)
    def _():
        for tap in range(CONV_KERNEL):
            cpar_ref[tap] = jnp.broadcast_to(cw_ref[tap:tap + 1, :], (SUBLANES, CONV_WIDTH))
        for i, par in enumerate((cb_ref, lnw_ref, lnb_ref)):
            cpar_ref[CONV_KERNEL + i] = jnp.broadcast_to(par[...], (SUBLANES, CONV_WIDTH))
        for _, item in _conv_pieces(u_ref, up_ref, un_ref, tile > 0, tile < n_tiles - 1, *conv_consts):
            item(None)

    mix_ref[:, R:R + CONV_WIDTH] = yconv_ref[...]

    lgf = lgf_ref[...]
    lgb = lgb_ref[...]
    row = lax.broadcasted_iota(jnp.int32, (CHUNK, R), 0).astype(F32)
    xi_f = jnp.exp(lgf * (row + 1.0))
    xi_b = jnp.exp(lgb * (float(CHUNK) - row))
    zeta_b = jnp.exp(lgb * row)
    decay_b = jnp.exp(lgb * float(CHUNK))
    ii = lax.broadcasted_iota(jnp.int32, (CHUNK, CHUNK), 0)
    jj = lax.broadcasted_iota(jnp.int32, (CHUNK, CHUNK), 1)
    diff = (ii - jj).astype(F32)
    dmask = jnp.concatenate(
        [jnp.where(diff >= 0.0,
                   jnp.exp(lgf[:, hd * HEAD_DIM:(hd + 1) * HEAD_DIM] * jnp.maximum(diff, 0.0)),
                   jnp.exp(lgb[:, hd * HEAD_DIM:(hd + 1) * HEAD_DIM] * jnp.maximum(-diff, 0.0)))
         for hd in range(RET_HEADS)], axis=1)

    gnw = gnw_ref[...]
    gnb = gnb_ref[...]
    rb = rb_ref[...]
    for c in reversed(range(t // CHUNK)):
        rows = slice(c * CHUNK, (c + 1) * CHUNK)
        qc = q_ref[0, rows, :]
        kc = k_ref[0, rows, :]
        vc = v_ref[0, rows, :]
        sgc = sg_ref[0, rows, :].astype(F32)
        rfc = rf_ref[0, c]
        rbb = rb.astype(BF16)
        qf = qc.astype(F32)
        qxf = (qf * xi_f).astype(BF16)
        qxb = (qf * xi_b).astype(BF16)
        kzb = (kc.astype(F32) * zeta_b).astype(BF16)
        ys = []
        upds = []
        for ps in (slice(0, 2 * HEAD_DIM), slice(2 * HEAD_DIM, 4 * HEAD_DIM)):
            p = (_dot_nt(qc[:, ps], _block_diag(kc[:, ps])) * dmask[:, ps]).astype(BF16)
            lhs = jnp.concatenate([p, qxf[:, ps], qxb[:, ps]], axis=1)
            rhs = jnp.concatenate([_block_diag(vc[:, ps]), _block_diag(rfc[:, ps]), _block_diag(rbb[:, ps])],
                                  axis=0)
            ys.append(_dot(lhs, rhs))
            kv = _dot_tn(kzb[:, ps], vc[:, ps])
            upds += [kv[:HEAD_DIM, :HEAD_DIM], kv[HEAD_DIM:, HEAD_DIM:]]
        yn = []
        for hd, y in enumerate(ys):
            for half in (y[:, :HEAD_DIM], y[:, HEAD_DIM:]):
                mu = jnp.mean(half, axis=-1, keepdims=True)
                yc = half - mu
                var = jnp.mean(yc * yc, axis=-1, keepdims=True)
                yn.append(yc * lax.rsqrt(var + EPS))
        yn = jnp.concatenate(yn, axis=1) * gnw + gnb
        mix_ref[rows, 0:R] = (sgc * yn).astype(BF16)
        rb = decay_b * rb + jnp.concatenate(upds, axis=1)
    rb_ref[...] = rb

    work = _conv_pieces(u1_ref, up1_ref, un1_ref, tile1 > 0, tile1 < n_tiles - 1, *conv_consts)
    total_cost = sum(cost for cost, _ in work)
    n_chunks = D_FF // FF_CHUNK
    n_shares = 3 + 2 * n_chunks - 1
    progress = {"share": 0, "item": 0, "cost": 0}

    def conv_share(result):
        progress["share"] += 1
        items = []
        while progress["item"] < len(work) and progress["cost"] * n_shares < total_cost * progress["share"]:
            cost, item = work[progress["item"]]
            items.append(item)
            progress["item"] += 1
            progress["cost"] += cost
        for item, gate in zip(items, _gates_from(result, len(items))):
            item(gate)
        return result

    x1 = x_ref[0] + conv_share(_dot(mix_ref[...], wout_ref[...]))

    qx = _dot((x1 * xnw_ref[...]).astype(BF16), wxq_ref[...])
    qx = (conv_share(qx) * _inv_rms(x1)).astype(BF16)
    outs = []
    for hd in range(XATTN_HEADS):
        hs = slice(hd * XATTN_HEAD_DIM, (hd + 1) * XATTN_HEAD_DIM)
        s = _dot(qx[:, hs], kt_ref[0, hs, :])
        e = jnp.exp(s - jnp.max(s, axis=-1, keepdims=True))
        p = e * (1.0 / jnp.sum(e, axis=-1, keepdims=True))
        outs.append(_dot(p.astype(BF16), vm_ref[0, :, hs]).astype(BF16))
    x2 = x1 + conv_share(_dot(jnp.concatenate(outs, axis=1), wxo_ref[...]))

    hm = (x2 * mnw_ref[...]).astype(BF16)
    acc = jnp.zeros((t, D_MODEL), F32)
    for c in range(n_chunks):
        cs = slice(c * FF_CHUNK, (c + 1) * FF_CHUNK)
        hc = conv_share(jnp.maximum(_dot(hm, w1_ref[:, cs]), 0.0))
        part = _dot((hc * hc).astype(BF16), w2_ref[cs, :])
        acc = acc + (conv_share(part) if c < n_chunks - 1 else part)
    assert progress["item"] == len(work) and progress["share"] == n_shares
    scale = _inv_rms(x2)
    o_ref[0] = _rms(x2 + acc * (scale * scale), fnw_ref[...])


def _block_rest(x, q, k, v, sg, u, rf, lgf_lanes, lgb_lanes, gn_w, gn_b, conv_w, conv_b, ln_w, ln_b,
                w_out, xattn_norm_w, kt, vm, w_xq, w_xo, mlp_norm_w, w_ff1, w_ff2, final_norm_w):
    b, s, d = x.shape
    t = REST_TILE
    R = RET_WIDTH
    n_tiles = s // t
    n_steps = b * n_tiles
    hb = t // HALO
    n_hb = s // HALO

    def batch_tile(step):
        return step // n_tiles, n_tiles - 1 - step % n_tiles

    def ahead(step):
        return jnp.minimum(step + 1, n_steps - 1)

    def main_map(step):
        bi, ti = batch_tile(step)
        return bi, ti, 0

    def prev_halo_map(step):
        bi, ti = batch_tile(step)
        return bi, jnp.maximum(ti * hb - 1, 0), 0

    def next_halo_map(step):
        bi, ti = batch_tile(step)
        return bi, jnp.minimum((ti + 1) * hb, n_hb - 1), 0

    act_spec = pl.BlockSpec((1, t, R), main_map)
    halo_shape = (1, HALO, R)
    m = kt.shape[2]
    return pl.pallas_call(
        functools.partial(_rest_body, n_tiles),
        grid=(n_steps,),
        in_specs=[pl.BlockSpec((1, t, d), main_map),
                  act_spec, act_spec, act_spec, act_spec, act_spec,
                  pl.BlockSpec(halo_shape, prev_halo_map),
                  pl.BlockSpec(halo_shape, next_halo_map),
                  pl.BlockSpec((1, t, R), lambda st: main_map(ahead(st))),
                  pl.BlockSpec(halo_shape, lambda st: prev_halo_map(ahead(st))),
                  pl.BlockSpec(halo_shape, lambda st: next_halo_map(ahead(st))),
                  pl.BlockSpec((1, t // CHUNK, CHUNK, R), lambda st: main_map(st) + (0,)),
                  _const_spec((1, R)), _const_spec((1, R)),
                  _const_spec((1, R)), _const_spec((1, R)),
                  _const_spec((CONV_KERNEL, CONV_WIDTH)),
                  _const_spec((1, CONV_WIDTH)), _const_spec((1, CONV_WIDTH)), _const_spec((1, CONV_WIDTH)),
                  _const_spec((d, d)),
                  _const_spec((1, d)),
                  pl.BlockSpec((1, d, m), lambda st: (st // n_tiles, 0, 0)),
                  pl.BlockSpec((1, m, d), lambda st: (st // n_tiles, 0, 0)),
                  _const_spec((d, d)), _const_spec((d, d)),
                  _const_spec((1, d)),
                  _const_spec((d, D_FF)), _const_spec((D_FF, d)),
                  _const_spec((1, d))],
        out_specs=pl.BlockSpec((1, t, d), main_map),
        out_shape=jax.ShapeDtypeStruct((b, s, d), F32),
        scratch_shapes=[pltpu.VMEM((CHUNK, R), F32),
                        pltpu.VMEM((t, D_MODEL), BF16),
                        pltpu.VMEM((t + 2 * HALO, CONV_WIDTH), F32),
                        pltpu.VMEM((2, SUBLANES, CONV_SPAN + SHIFT_ROWS, CONV_WIDTH), F32),
                        pltpu.VMEM((t, CONV_WIDTH), BF16),
                        pltpu.VMEM((CONV_KERNEL + 3, SUBLANES, CONV_WIDTH), F32)],
        compiler_params=pltpu.CompilerParams(
            dimension_semantics=("arbitrary",), vmem_limit_bytes=VMEM_LIMIT_BYTES),
        name="block_rest",
    )(x, q, k, v, sg, u, u, u, u, u, u, rf, lgf_lanes, lgb_lanes, gn_w, gn_b, conv_w, conv_b, ln_w, ln_b,
      w_out, xattn_norm_w, kt, vm, w_xq, w_xo, mlp_norm_w, w_ff1, w_ff2, final_norm_w)


def kernel(x, mem, positions, mix_norm_w, w_in, ret_decay_f, ret_decay_b, ret_gn_w, ret_gn_b,
           conv_w, conv_b, conv_ln_w, conv_ln_b, w_out, xattn_norm_w, mem_norm_w,
           w_xq, w_xkv, w_xo, mlp_norm_w, w_ff1, w_ff2, final_norm_w):
    assert w_in.shape[0] == 1, "single-layer block"
    b, s, d = x.shape
    half = HEAD_DIM // 2
    inv_freq = ROPE_THETA ** (-jnp.arange(half, dtype=F32) / half)
    invf2 = jnp.concatenate([inv_freq, inv_freq])[:, None]
    pos3 = positions.reshape(b, 1, s)
    row = lambda a: a.reshape(1, -1)
    lgf_lanes = jnp.repeat(jax.nn.log_sigmoid(ret_decay_f[0].astype(F32)), HEAD_DIM)[None]
    lgb_lanes = jnp.repeat(jax.nn.log_sigmoid(ret_decay_b[0].astype(F32)), HEAD_DIM)[None]
    kt, vm = _mem_kv(mem, row(mem_norm_w[0]), w_xkv[0])
    (q, k, v, sg, u, rf, w_out_b, w_xq_b, w_xo_b, w_ff1_b, w_ff2_b) = _in_proj(
        x, pos3, row(mix_norm_w[0]), invf2, lgf_lanes, w_in[0],
        (w_out[0], w_xq[0], w_xo[0], w_ff1[0], w_ff2[0]))
    return _block_rest(x, q, k, v, sg, u, rf, lgf_lanes, lgb_lanes,
                       row(ret_gn_w[0]), row(ret_gn_b[0]),
                       conv_w[0].reshape(CONV_KERNEL, CONV_WIDTH), row(conv_b[0]),
                       row(conv_ln_w[0]), row(conv_ln_b[0]),
                       w_out_b, row(xattn_norm_w[0]), kt, vm, w_xq_b, w_xo_b,
                       row(mlp_norm_w[0]), w_ff1_b, w_ff2_b, row(final_norm_w))
```

```python
import functools

import jax
import jax.numpy as jnp
import numpy as np
from jax import lax
from jax.experimental import pallas as pl
from jax.experimental.pallas import tpu as pltpu

F32 = jnp.float32
BF16 = jnp.bfloat16

D_MODEL = 1024
RET_WIDTH = 512
CONV_WIDTH = 512
RET_HEADS = 4
HEAD_DIM = 128
CHUNK = 128
CONV_KERNEL = 31
CONV_PAD = CONV_KERNEL // 2
HALO = 16
XATTN_HEADS = 4
XATTN_HEAD_DIM = 256
D_FF = 4096
ROPE_THETA = 10000.0
EPS = 1e-6

IN_PROJ_TILE = 1024
REST_TILE = 512
FF_CHUNK = 512
CONV_ROWS = 32
CONV_SPAN = 128
CAST_ROWS = 128
HALF_PI_1 = 1.5703125
HALF_PI_2 = 4.837512969970703125e-4
HALF_PI_3 = 7.54978995489188216e-8
SUBLANES = 8
BF16_SUBLANES = 16
LANES = 128
SHIFT_ROWS = ((HALO - CONV_PAD + CONV_KERNEL - 1) // SUBLANES) * SUBLANES
VMEM_LIMIT_BYTES = 58 * 1024 * 1024


def _inv_rms(x):
    return lax.rsqrt(jnp.mean(x * x, axis=-1, keepdims=True) + EPS)


def _rms(x, w):
    return x * _inv_rms(x) * w


def _dot(a, b):
    return jnp.dot(a, b, preferred_element_type=F32)


def _dot_tn(a, b):
    return lax.dot_general(a, b, (((0,), (0,)), ((), ())), preferred_element_type=F32)


def _dot_nt(a, b):
    return lax.dot_general(a, b, (((1,), (1,)), ((), ())), preferred_element_type=F32)


def _block_diag(a):
    n, w2 = a.shape
    zero = jnp.zeros((n, w2 // 2), a.dtype)
    return jnp.concatenate([jnp.concatenate([a[:, :w2 // 2], zero], axis=1),
                            jnp.concatenate([zero, a[:, w2 // 2:]], axis=1)], axis=0)


def _const_spec(shape):
    nd = len(shape)
    return pl.BlockSpec(shape, lambda *_: (0,) * nd, pipeline_mode=pl.Buffered(1))


def _mem_kv_body(mem_ref, nw_ref, wkv_ref, kt_ref, v_ref):
    m = _rms(mem_ref[0], nw_ref[...]).astype(BF16)
    kv = _dot(m, wkv_ref[...].astype(BF16))
    k = kv[:, :D_MODEL] * (XATTN_HEAD_DIM ** -0.5)
    kt_ref[0] = k.T.astype(BF16)
    v_ref[0] = kv[:, D_MODEL:].astype(BF16)


def _mem_kv(mem, mem_norm_w, w_xkv):
    b, m, d = mem.shape
    return pl.pallas_call(
        _mem_kv_body,
        grid=(b,),
        in_specs=[pl.BlockSpec((1, m, d), lambda i: (i, 0, 0)),
                  _const_spec((1, d)),
                  _const_spec((d, 2 * d))],
        out_specs=[pl.BlockSpec((1, d, m), lambda i: (i, 0, 0)),
                   pl.BlockSpec((1, m, d), lambda i: (i, 0, 0))],
        out_shape=[jax.ShapeDtypeStruct((b, d, m), BF16),
                   jax.ShapeDtypeStruct((b, m, d), BF16)],
        compiler_params=pltpu.CompilerParams(
            dimension_semantics=("arbitrary",), vmem_limit_bytes=VMEM_LIMIT_BYTES),
        name="mem_kv",
    )(mem, mem_norm_w, w_xkv)


def _cos_or_sin(x, want_sin):
    k = jnp.floor(x * (2.0 / np.pi) + 0.5)
    r = ((x - k * HALF_PI_1) - k * HALF_PI_2) - k * HALF_PI_3
    r2 = r * r
    sin_r = r + r * r2 * (-1.6666654611e-1 + r2 * (8.3321608736e-3 + r2 * -1.9515295891e-4))
    cos_r = 1.0 - 0.5 * r2 + r2 * r2 * (4.166664568298827e-2
                                        + r2 * (-1.388731625493765e-3 + r2 * 2.443315711809948e-5))
    kq = k + jnp.where(want_sin, 3.0, 0.0)
    q = kq - 4.0 * jnp.floor(kq * 0.25)
    hi = jnp.floor(q * 0.5)
    lo = q - 2.0 * hi
    return (1.0 - 2.0 * hi) * (cos_r - lo * (cos_r + sin_r))


def _in_proj_body(x_ref, pos_ref, nw_ref, invf_ref, lgf_ref, w_ref,
                  wout_ref, wxq_ref, wxo_ref, w1_ref, w2_ref,
                  q_ref, k_ref, v_ref, sg_ref, u_ref, rf_ref,
                  wout_o, wxq_o, wxo_o, w1_o, w2_o, state_ref, wbf_ref):
    t = x_ref.shape[1]
    R = RET_WIDTH

    @pl.when((pl.program_id(0) == 0) & (pl.program_id(1) == 0))
    def _():
        for r0 in range(0, D_MODEL, CAST_ROWS):
            wbf_ref[r0:r0 + CAST_ROWS, :] = w_ref[r0:r0 + CAST_ROWS, :].astype(BF16)

    @pl.when(pl.program_id(1) == 0)
    def _():
        state_ref[...] = jnp.zeros_like(state_ref)

    for src, dst in ((wout_ref, wout_o), (wxq_ref, wxq_o), (wxo_ref, wxo_o), (w1_ref, w1_o), (w2_ref, w2_o)):
        dst[...] = src[...].astype(BF16)

    h = _rms(x_ref[0], nw_ref[...]).astype(BF16)

    ang_t = invf_ref[...] * pos_ref[0].astype(F32)
    cs = _cos_or_sin(ang_t, lax.broadcasted_iota(jnp.int32, (HEAD_DIM, 1), 0) >= HEAD_DIM // 2).T
    upper = lax.broadcasted_iota(jnp.int32, (1, HEAD_DIM), 1) >= HEAD_DIM // 2
    sc = pltpu.roll(cs, HEAD_DIM // 2, 1)
    cos2 = jnp.where(upper, sc, cs)
    sin2 = jnp.where(upper, cs, -sc)

    def rope(p):
        outs = []
        for hd in range(RET_HEADS):
            ph = p[:, hd * HEAD_DIM:(hd + 1) * HEAD_DIM]
            outs.append(ph * cos2 + pltpu.roll(ph, HEAD_DIM // 2, 1) * sin2)
        return jnp.concatenate(outs, axis=1)

    q = rope(_dot(h, wbf_ref[:, 0:R]))
    q_ref[0] = q.astype(BF16)
    k = rope(_dot(h, wbf_ref[:, R:2 * R])) * (HEAD_DIM ** -0.5)
    k_ref[0] = k.astype(BF16)
    v = _dot(h, wbf_ref[:, 2 * R:3 * R]).astype(BF16)
    v_ref[0] = v
    g = _dot(h, wbf_ref[:, 3 * R:4 * R])
    sg_ref[0] = (g * jax.nn.sigmoid(g)).astype(BF16)
    a = _dot(h, wbf_ref[:, 4 * R:5 * R])
    bb = _dot(h, wbf_ref[:, 5 * R:6 * R])
    u_ref[0] = (a * jax.nn.sigmoid(bb)).astype(BF16)

    lgf = lgf_ref[...]
    row = lax.broadcasted_iota(jnp.int32, (CHUNK, R), 0).astype(F32)
    zeta = jnp.exp(lgf * (CHUNK - 1.0 - row))
    chunk_decay = jnp.exp(lgf * float(CHUNK))
    state = state_ref[...]
    for c in range(t // CHUNK):
        rf_ref[0, c] = state.astype(BF16)
        rows = slice(c * CHUNK, (c + 1) * CHUNK)
        kz = (k[rows] * zeta).astype(BF16)
        vc = v[rows]
        upd = jnp.concatenate(
            [_dot_tn(kz[:, hd * HEAD_DIM:(hd + 1) * HEAD_DIM],
                     vc[:, hd * HEAD_DIM:(hd + 1) * HEAD_DIM]) for hd in range(RET_HEADS)], axis=1)
        state = chunk_decay * state + upd
    state_ref[...] = state


def _in_proj(x, pos3, mix_norm_w, invf2, lgf_lanes, w_in, later_weights):
    b, s, d = x.shape
    t = IN_PROJ_TILE
    R = RET_WIDTH
    n_j = s // t
    n_steps = b * n_j
    tok = lambda i, j: (i, j, 0)
    act_spec = pl.BlockSpec((1, t, R), tok)
    act_shape = jax.ShapeDtypeStruct((b, s, R), BF16)
    cast_specs = []
    for w in later_weights:
        rows, cols = w.shape
        assert rows % (n_steps * BF16_SUBLANES) == 0
        cast_specs.append(pl.BlockSpec((rows // n_steps, cols), lambda i, j: (i * n_j + j, 0)))
    return pl.pallas_call(
        _in_proj_body,
        grid=(b, n_j),
        in_specs=[pl.BlockSpec((1, t, d), tok),
                  pl.BlockSpec((1, 1, t), lambda i, j: (i, 0, j)),
                  _const_spec((1, d)),
                  _const_spec((HEAD_DIM, 1)),
                  _const_spec((1, R)),
                  _const_spec((d, 6 * R))] + cast_specs,
        out_specs=([act_spec] * 5 + [pl.BlockSpec((1, t // CHUNK, CHUNK, R), lambda i, j: (i, j, 0, 0))]
                   + cast_specs),
        out_shape=([act_shape] * 5 + [jax.ShapeDtypeStruct((b, s // CHUNK, CHUNK, R), BF16)]
                   + [jax.ShapeDtypeStruct(w.shape, BF16) for w in later_weights]),
        scratch_shapes=[pltpu.VMEM((CHUNK, R), F32),
                        pltpu.VMEM((d, 6 * R), BF16)],
        compiler_params=pltpu.CompilerParams(
            dimension_semantics=("arbitrary", "arbitrary"), vmem_limit_bytes=VMEM_LIMIT_BYTES),
        name="in_proj",
    )(x, pos3, mix_norm_w, invf2, lgf_lanes, w_in, *later_weights)


def _conv_pieces(u_ref, up_ref, un_ref, has_prev, has_next,
                 cpar_ref, uext_ref, ush_ref, yconv_ref):
    t = u_ref.shape[1]
    span = CONV_SPAN + SHIFT_ROWS
    groups = (CONV_ROWS // SUBLANES, SUBLANES, CONV_WIDTH)
    block_vregs = CONV_ROWS * CONV_WIDTH // (SUBLANES * LANES)
    span_vregs = span * CONV_WIDTH // (SUBLANES * LANES)
    accs = {}

    def fill(gate):
        del gate
        zero_halo = jnp.zeros((HALO, CONV_WIDTH), F32)
        uext_ref[0:HALO, :] = jnp.where(has_prev, up_ref[0].astype(F32), zero_halo)
        uext_ref[HALO:HALO + t, :] = u_ref[0].astype(F32)
        uext_ref[HALO + t:HALO + t + HALO, :] = jnp.where(has_next, un_ref[0].astype(F32), zero_halo)

    def shift(hb, sft, gate):
        ush_ref[(hb // CONV_SPAN) % 2, sft, 0:span, :] = _gated(uext_ref[hb + sft:hb + sft + span, :], gate)

    def tap_step(hb, base, tap, gate):
        grp, sft = divmod(HALO - CONV_PAD + tap, SUBLANES)
        lo = base + grp * SUBLANES
        if sft == 0:
            src = uext_ref[hb + lo:hb + lo + CONV_ROWS, :]
        else:
            src = ush_ref[(hb // CONV_SPAN) % 2, sft, lo:lo + CONV_ROWS, :]
        acc = accs.pop((hb, base)) if tap else jnp.broadcast_to(cpar_ref[CONV_KERNEL][None], groups)
        weight = cpar_ref[tap]
        weight = _gated(weight, gate)
        acc = acc + weight[None] * src.reshape(groups)
        if tap < CONV_KERNEL - 1:
            accs[(hb, base)] = acc
            return
        mu = jnp.mean(acc, axis=-1, keepdims=True)
        yc = acc - mu
        var = jnp.mean(yc * yc, axis=-1, keepdims=True)
        yl = yc * lax.rsqrt(var + EPS) * cpar_ref[CONV_KERNEL + 1][None] + cpar_ref[CONV_KERNEL + 2][None]
        out = (yl * jax.nn.sigmoid(yl)).reshape(CONV_ROWS, CONV_WIDTH)
        yconv_ref[hb + base:hb + base + CONV_ROWS, :] = out.astype(BF16)

    def copies(hb):
        return [(3 * span_vregs, functools.partial(shift, hb, sft)) for sft in range(1, SUBLANES)]

    def taps(hb):
        return [(2 * block_vregs + (12 * block_vregs if tap == CONV_KERNEL - 1 else 0),
                 functools.partial(tap_step, hb, base, tap))
                for base in range(0, CONV_SPAN, CONV_ROWS) for tap in range(CONV_KERNEL)]

    work = [(2 * (t + 2 * HALO) * CONV_WIDTH // (SUBLANES * LANES), fill)] + copies(0)
    for hb in range(0, t, CONV_SPAN):
        ahead = copies(hb + CONV_SPAN) if hb + CONV_SPAN < t else []
        steps = taps(hb)
        every = len(steps) // (len(ahead) + 1)
        for i, step in enumerate(steps):
            work.append(step)
            if ahead and (i + 1) % every == 0:
                work.append(ahead.pop(0))
        assert not ahead
    return work


def _gated(x, gate):
    if gate is None:
        return x
    rows, cols = x.shape
    return x + jnp.tile(gate, (rows // SUBLANES, cols // LANES))


def _gates_from(y, n):
    rows, cols = y.shape
    tiles = [(r, c) for c in range(0, cols, LANES) for r in range(0, rows, SUBLANES)]
    picks = [tiles[(i * len(tiles)) // n] for i in range(n)]
    return [pltpu.bitcast((pltpu.bitcast(y[r:r + SUBLANES, c:c + LANES], jnp.uint32) >> 16) >> 16, F32)
            for r, c in picks]


def _rest_body(n_tiles,
               x_ref, q_ref, k_ref, v_ref, sg_ref, u_ref, up_ref, un_ref, u1_ref, up1_ref, un1_ref, rf_ref,
               lgf_ref, lgb_ref, gnw_ref, gnb_ref, cw_ref, cb_ref, lnw_ref, lnb_ref,
               wout_ref, xnw_ref, kt_ref, vm_ref, wxq_ref, wxo_ref,
               mnw_ref, w1_ref, w2_ref, fnw_ref,
               o_ref, rb_ref, mix_ref, uext_ref, ush_ref, yconv_ref, cpar_ref):
    t = x_ref.shape[1]
    R = RET_WIDTH
    step = pl.program_id(0)
    last_step = pl.num_programs(0) - 1
    tile = n_tiles - 1 - step % n_tiles
    step1 = jnp.minimum(step + 1, last_step)
    tile1 = n_tiles - 1 - step1 % n_tiles
    conv_consts = (cpar_ref, uext_ref, ush_ref, yconv_ref)

    @pl.when(step % n_tiles == 0)
    def _():
        rb_ref[...] = jnp.zeros_like(rb_ref)

    @pl.when(step == 0)
    def _():
        for tap in range(CONV_KERNEL):
            cpar_ref[tap] = jnp.broadcast_to(cw_ref[tap:tap + 1, :], (SUBLANES, CONV_WIDTH))
        for i, par in enumerate((cb_ref, lnw_ref, lnb_ref)):
            cpar_ref[CONV_KERNEL + i] = jnp.broadcast_to(par[...], (SUBLANES, CONV_WIDTH))
        for _, item in _conv_pieces(u_ref, up_ref, un_ref, tile > 0, tile < n_tiles - 1, *conv_consts):
            item(None)

    mix_ref[:, R:R + CONV_WIDTH] = yconv_ref[...]

    lgf = lgf_ref[...]
    lgb = lgb_ref[...]
    row = lax.broadcasted_iota(jnp.int32, (CHUNK, R), 0).astype(F32)
    xi_f = jnp.exp(lgf * (row + 1.0))
    xi_b = jnp.exp(lgb * (float(CHUNK) - row))
    zeta_b = jnp.exp(lgb * row)
    decay_b = jnp.exp(lgb * float(CHUNK))
    ii = lax.broadcasted_iota(jnp.int32, (CHUNK, CHUNK), 0)
    jj = lax.broadcasted_iota(jnp.int32, (CHUNK, CHUNK), 1)
    diff = (ii - jj).astype(F32)
    dmask = jnp.concatenate(
        [jnp.where(diff >= 0.0,
                   jnp.exp(lgf[:, hd * HEAD_DIM:(hd + 1) * HEAD_DIM] * jnp.maximum(diff, 0.0)),
                   jnp.exp(lgb[:, hd * HEAD_DIM:(hd + 1) * HEAD_DIM] * jnp.maximum(-diff, 0.0)))
         for hd in range(RET_HEADS)], axis=1)

    gnw = gnw_ref[...]
    gnb = gnb_ref[...]
    rb = rb_ref[...]
    for c in reversed(range(t // CHUNK)):
        rows = slice(c * CHUNK, (c + 1) * CHUNK)
        qc = q_ref[0, rows, :]
        kc = k_ref[0, rows, :]
        vc = v_ref[0, rows, :]
        sgc = sg_ref[0, rows, :].astype(F32)
        rfc = rf_ref[0, c]
        rbb = rb.astype(BF16)
        qf = qc.astype(F32)
        qxf = (qf * xi_f).astype(BF16)
        qxb = (qf * xi_b).astype(BF16)
        kzb = (kc.astype(F32) * zeta_b).astype(BF16)
        ys = []
        upds = []
        for ps in (slice(0, 2 * HEAD_DIM), slice(2 * HEAD_DIM, 4 * HEAD_DIM)):
            p = (_dot_nt(qc[:, ps], _block_diag(kc[:, ps])) * dmask[:, ps]).astype(BF16)
            lhs = jnp.concatenate([p, qxf[:, ps], qxb[:, ps]], axis=1)
            rhs = jnp.concatenate([_block_diag(vc[:, ps]), _block_diag(rfc[:, ps]), _block_diag(rbb[:, ps])],
                                  axis=0)
            ys.append(_dot(lhs, rhs))
            kv = _dot_tn(kzb[:, ps], vc[:, ps])
            upds += [kv[:HEAD_DIM, :HEAD_DIM], kv[HEAD_DIM:, HEAD_DIM:]]
        yn = []
        for hd, y in enumerate(ys):
            for half in (y[:, :HEAD_DIM], y[:, HEAD_DIM:]):
                mu = jnp.mean(half, axis=-1, keepdims=True)
                yc = half - mu
                var = jnp.mean(yc * yc, axis=-1, keepdims=True)
                yn.append(yc * lax.rsqrt(var + EPS))
        yn = jnp.concatenate(yn, axis=1) * gnw + gnb
        mix_ref[rows, 0:R] = (sgc * yn).astype(BF16)
        rb = decay_b * rb + jnp.concatenate(upds, axis=1)
    rb_ref[...] = rb

    work = _conv_pieces(u1_ref, up1_ref, un1_ref, tile1 > 0, tile1 < n_tiles - 1, *conv_consts)
    total_cost = sum(cost for cost, _ in work)
    n_chunks = D_FF // FF_CHUNK
    n_shares = 3 + 2 * n_chunks - 1
    progress = {"share": 0, "item": 0, "cost": 0}

    def conv_share(result):
        progress["share"] += 1
        items = []
        while progress["item"] < len(work) and progress["cost"] * n_shares < total_cost * progress["share"]:
            cost, item = work[progress["item"]]
            items.append(item)
            progress["item"] += 1
            progress["cost"] += cost
        for item, gate in zip(items, _gates_from(result, len(items))):
            item(gate)
        return result

    x1 = x_ref[0] + conv_share(_dot(mix_ref[...], wout_ref[...]))

    qx = _dot((x1 * xnw_ref[...]).astype(BF16), wxq_ref[...])
    qx = (conv_share(qx) * _inv_rms(x1)).astype(BF16)
    outs = []
    for hd in range(XATTN_HEADS):
        hs = slice(hd * XATTN_HEAD_DIM, (hd + 1) * XATTN_HEAD_DIM)
        s = _dot(qx[:, hs], kt_ref[0, hs, :])
        e = jnp.exp(s - jnp.max(s, axis=-1, keepdims=True))
        p = e * (1.0 / jnp.sum(e, axis=-1, keepdims=True))
        outs.append(_dot(p.astype(BF16), vm_ref[0, :, hs]).astype(BF16))
    x2 = x1 + conv_share(_dot(jnp.concatenate(outs, axis=1), wxo_ref[...]))

    hm = (x2 * mnw_ref[...]).astype(BF16)
    acc = jnp.zeros((t, D_MODEL), F32)
    for c in range(n_chunks):
        cs = slice(c * FF_CHUNK, (c + 1) * FF_CHUNK)
        hc = conv_share(jnp.maximum(_dot(hm, w1_ref[:, cs]), 0.0))
        part = _dot((hc * hc).astype(BF16), w2_ref[cs, :])
        acc = acc + (conv_share(part) if c < n_chunks - 1 else part)
    assert progress["item"] == len(work) and progress["share"] == n_shares
    scale = _inv_rms(x2)
    o_ref[0] = _rms(x2 + acc * (scale * scale), fnw_ref[...])


def _block_rest(x, q, k, v, sg, u, rf, lgf_lanes, lgb_lanes, gn_w, gn_b, conv_w, conv_b, ln_w, ln_b,
                w_out, xattn_norm_w, kt, vm, w_xq, w_xo, mlp_norm_w, w_ff1, w_ff2, final_norm_w):
    b, s, d = x.shape
    t = REST_TILE
    R = RET_WIDTH
    n_tiles = s // t
    n_steps = b * n_tiles
    hb = t // HALO
    n_hb = s // HALO

    def batch_tile(step):
        return step // n_tiles, n_tiles - 1 - step % n_tiles

    def ahead(step):
        return jnp.minimum(step + 1, n_steps - 1)

    def main_map(step):
        bi, ti = batch_tile(step)
        return bi, ti, 0

    def prev_halo_map(step):
        bi, ti = batch_tile(step)
        return bi, jnp.maximum(ti * hb - 1, 0), 0

    def next_halo_map(step):
        bi, ti = batch_tile(step)
        return bi, jnp.minimum((ti + 1) * hb, n_hb - 1), 0

    act_spec = pl.BlockSpec((1, t, R), main_map)
    halo_shape = (1, HALO, R)
    m = kt.shape[2]
    return pl.pallas_call(
        functools.partial(_rest_body, n_tiles),
        grid=(n_steps,),
        in_specs=[pl.BlockSpec((1, t, d), main_map),
                  act_spec, act_spec, act_spec, act_spec, act_spec,
                  pl.BlockSpec(halo_shape, prev_halo_map),
                  pl.BlockSpec(halo_shape, next_halo_map),
                  pl.BlockSpec((1, t, R), lambda st: main_map(ahead(st))),
                  pl.BlockSpec(halo_shape, lambda st: prev_halo_map(ahead(st))),
                  pl.BlockSpec(halo_shape, lambda st: next_halo_map(ahead(st))),
                  pl.BlockSpec((1, t // CHUNK, CHUNK, R), lambda st: main_map(st) + (0,)),
                  _const_spec((1, R)), _const_spec((1, R)),
                  _const_spec((1, R)), _const_spec((1, R)),
                  _const_spec((CONV_KERNEL, CONV_WIDTH)),
                  _const_spec((1, CONV_WIDTH)), _const_spec((1, CONV_WIDTH)), _const_spec((1, CONV_WIDTH)),
                  _const_spec((d, d)),
                  _const_spec((1, d)),
                  pl.BlockSpec((1, d, m), lambda st: (st // n_tiles, 0, 0)),
                  pl.BlockSpec((1, m, d), lambda st: (st // n_tiles, 0, 0)),
                  _const_spec((d, d)), _const_spec((d, d)),
                  _const_spec((1, d)),
                  _const_spec((d, D_FF)), _const_spec((D_FF, d)),
                  _const_spec((1, d))],
        out_specs=pl.BlockSpec((1, t, d), main_map),
        out_shape=jax.ShapeDtypeStruct((b, s, d), F32),
        scratch_shapes=[pltpu.VMEM((CHUNK, R), F32),
                        pltpu.VMEM((t, D_MODEL), BF16),
                        pltpu.VMEM((t + 2 * HALO, CONV_WIDTH), F32),
                        pltpu.VMEM((2, SUBLANES, CONV_SPAN + SHIFT_ROWS, CONV_WIDTH), F32),
                        pltpu.VMEM((t, CONV_WIDTH), BF16),
                        pltpu.VMEM((CONV_KERNEL + 3, SUBLANES, CONV_WIDTH), F32)],
        compiler_params=pltpu.CompilerParams(
            dimension_semantics=("arbitrary",), vmem_limit_bytes=VMEM_LIMIT_BYTES),
        name="block_rest",
    )(x, q, k, v, sg, u, u, u, u, u, u, rf, lgf_lanes, lgb_lanes, gn_w, gn_b, conv_w, conv_b, ln_w, ln_b,
      w_out, xattn_norm_w, kt, vm, w_xq, w_xo, mlp_norm_w, w_ff1, w_ff2, final_norm_w)


def kernel(x, mem, positions, mix_norm_w, w_in, ret_decay_f, ret_decay_b, ret_gn_w, ret_gn_b,
           conv_w, conv_b, conv_ln_w, conv_ln_b, w_out, xattn_norm_w, mem_norm_w,
           w_xq, w_xkv, w_xo, mlp_norm_w, w_ff1, w_ff2, final_norm_w):
    assert w_in.shape[0] == 1, "single-layer block"
    b, s, d = x.shape
    half = HEAD_DIM // 2
    inv_freq = ROPE_THETA ** (-jnp.arange(half, dtype=F32) / half)
    invf2 = jnp.concatenate([inv_freq, inv_freq])[:, None]
    pos3 = positions.reshape(b, 1, s)
    row = lambda a: a.reshape(1, -1)
    lgf_lanes = jnp.repeat(jax.nn.log_sigmoid(ret_decay_f[0].astype(F32)), HEAD_DIM)[None]
    lgb_lanes = jnp.repeat(jax.nn.log_sigmoid(ret_decay_b[0].astype(F32)), HEAD_DIM)[None]
    kt, vm = _mem_kv(mem, row(mem_norm_w[0]), w_xkv[0])
    (q, k, v, sg, u, rf, w_out_b, w_xq_b, w_xo_b, w_ff1_b, w_ff2_b) = _in_proj(
        x, pos3, row(mix_norm_w[0]), invf2, lgf_lanes, w_in[0],
        (w_out[0], w_xq[0], w_xo[0], w_ff1[0], w_ff2[0]))
    return _block_rest(x, q, k, v, sg, u, rf, lgf_lanes, lgb_lanes,
                       row(ret_gn_w[0]), row(ret_gn_b[0]),
                       conv_w[0].reshape(CONV_KERNEL, CONV_WIDTH), row(conv_b[0]),
                       row(conv_ln_w[0]), row(conv_ln_b[0]),
                       w_out_b, row(xattn_norm_w[0]), kt, vm, w_xq_b, w_xo_b,
                       row(mlp_norm_w[0]), w_ff1_b, w_ff2_b, row(final_norm_w))
```

```python
import functools

import jax
import jax.numpy as jnp
import numpy as np
from jax import lax
from jax.experimental import pallas as pl
from jax.experimental.pallas import tpu as pltpu

F32 = jnp.float32
BF16 = jnp.bfloat16

D_MODEL = 1024
RET_WIDTH = 512
CONV_WIDTH = 512
RET_HEADS = 4
HEAD_DIM = 128
CHUNK = 128
CONV_KERNEL = 31
CONV_PAD = CONV_KERNEL // 2
HALO = 16
XATTN_HEADS = 4
XATTN_HEAD_DIM = 256
D_FF = 4096
ROPE_THETA = 10000.0
EPS = 1e-6

IN_PROJ_TILE = 1024
REST_TILE = 512
FF_CHUNK = 512
CONV_ROWS = 32
CONV_SPAN = 128
CAST_ROWS = 128
HALF_PI_1 = 1.5703125
HALF_PI_2 = 4.837512969970703125e-4
HALF_PI_3 = 7.54978995489188216e-8
SUBLANES = 8
BF16_SUBLANES = 16
LANES = 128
SHIFT_ROWS = ((HALO - CONV_PAD + CONV_KERNEL - 1) // SUBLANES) * SUBLANES
VMEM_LIMIT_BYTES = 58 * 1024 * 1024


def _inv_rms(x):
    return lax.rsqrt(jnp.mean(x * x, axis=-1, keepdims=True) + EPS)


def _rms(x, w):
    return x * _inv_rms(x) * w


def _dot(a, b):
    return jnp.dot(a, b, preferred_element_type=F32)


def _dot_tn(a, b):
    return lax.dot_general(a, b, (((0,), (0,)), ((), ())), preferred_element_type=F32)


def _dot_nt(a, b):
    return lax.dot_general(a, b, (((1,), (1,)), ((), ())), preferred_element_type=F32)


def _block_diag(a):
    n, w2 = a.shape
    zero = jnp.zeros((n, w2 // 2), a.dtype)
    return jnp.concatenate([jnp.concatenate([a[:, :w2 // 2], zero], axis=1),
                            jnp.concatenate([zero, a[:, w2 // 2:]], axis=1)], axis=0)


def _const_spec(shape):
    nd = len(shape)
    return pl.BlockSpec(shape, lambda *_: (0,) * nd, pipeline_mode=pl.Buffered(1))


def _mem_kv_body(mem_ref, nw_ref, wkv_ref, kt_ref, v_ref):
    m = _rms(mem_ref[0], nw_ref[...]).astype(BF16)
    kv = _dot(m, wkv_ref[...].astype(BF16))
    k = kv[:, :D_MODEL] * (XATTN_HEAD_DIM ** -0.5)
    kt_ref[0] = k.T.astype(BF16)
    v_ref[0] = kv[:, D_MODEL:].astype(BF16)


def _mem_kv(mem, mem_norm_w, w_xkv):
    b, m, d = mem.shape
    return pl.pallas_call(
        _mem_kv_body,
        grid=(b,),
        in_specs=[pl.BlockSpec((1, m, d), lambda i: (i, 0, 0)),
                  _const_spec((1, d)),
                  _const_spec((d, 2 * d))],
        out_specs=[pl.BlockSpec((1, d, m), lambda i: (i, 0, 0)),
                   pl.BlockSpec((1, m, d), lambda i: (i, 0, 0))],
        out_shape=[jax.ShapeDtypeStruct((b, d, m), BF16),
                   jax.ShapeDtypeStruct((b, m, d), BF16)],
        compiler_params=pltpu.CompilerParams(
            dimension_semantics=("arbitrary",), vmem_limit_bytes=VMEM_LIMIT_BYTES),
        name="mem_kv",
    )(mem, mem_norm_w, w_xkv)


def _cos_or_sin(x, want_sin):
    k = jnp.floor(x * (2.0 / np.pi) + 0.5)
    r = ((x - k * HALF_PI_1) - k * HALF_PI_2) - k * HALF_PI_3
    r2 = r * r
    sin_r = r + r * r2 * (-1.6666654611e-1 + r2 * (8.3321608736e-3 + r2 * -1.9515295891e-4))
    cos_r = 1.0 - 0.5 * r2 + r2 * r2 * (4.166664568298827e-2
                                        + r2 * (-1.388731625493765e-3 + r2 * 2.443315711809948e-5))
    kq = k + jnp.where(want_sin, 3.0, 0.0)
    q = kq - 4.0 * jnp.floor(kq * 0.25)
    hi = jnp.floor(q * 0.5)
    lo = q - 2.0 * hi
    return (1.0 - 2.0 * hi) * (cos_r - lo * (cos_r + sin_r))


def _in_proj_body(x_ref, pos_ref, nw_ref, invf_ref, lgf_ref, w_ref,
                  wout_ref, wxq_ref, wxo_ref, w1_ref, w2_ref,
                  q_ref, k_ref, v_ref, sg_ref, u_ref, rf_ref,
                  wout_o, wxq_o, wxo_o, w1_o, w2_o, state_ref, wbf_ref):
    t = x_ref.shape[1]
    R = RET_WIDTH

    @pl.when((pl.program_id(0) == 0) & (pl.program_id(1) == 0))
    def _():
        for r0 in range(0, D_MODEL, CAST_ROWS):
            wbf_ref[r0:r0 + CAST_ROWS, :] = w_ref[r0:r0 + CAST_ROWS, :].astype(BF16)

    @pl.when(pl.program_id(1) == 0)
    def _():
        state_ref[...] = jnp.zeros_like(state_ref)

    for src, dst in ((wout_ref, wout_o), (wxq_ref, wxq_o), (wxo_ref, wxo_o), (w1_ref, w1_o), (w2_ref, w2_o)):
        dst[...] = src[...].astype(BF16)

    h = _rms(x_ref[0], nw_ref[...]).astype(BF16)

    ang_t = invf_ref[...] * pos_ref[0].astype(F32)
    cs = _cos_or_sin(ang_t, lax.broadcasted_iota(jnp.int32, (HEAD_DIM, 1), 0) >= HEAD_DIM // 2).T
    upper = lax.broadcasted_iota(jnp.int32, (1, HEAD_DIM), 1) >= HEAD_DIM // 2
    sc = pltpu.roll(cs, HEAD_DIM // 2, 1)
    cos2 = jnp.where(upper, sc, cs)
    sin2 = jnp.where(upper, cs, -sc)

    def rope(p):
        outs = []
        for hd in range(RET_HEADS):
            ph = p[:, hd * HEAD_DIM:(hd + 1) * HEAD_DIM]
            outs.append(ph * cos2 + pltpu.roll(ph, HEAD_DIM // 2, 1) * sin2)
        return jnp.concatenate(outs, axis=1)

    q = rope(_dot(h, wbf_ref[:, 0:R]))
    q_ref[0] = q.astype(BF16)
    k = rope(_dot(h, wbf_ref[:, R:2 * R])) * (HEAD_DIM ** -0.5)
    k_ref[0] = k.astype(BF16)
    v = _dot(h, wbf_ref[:, 2 * R:3 * R]).astype(BF16)
    v_ref[0] = v
    g = _dot(h, wbf_ref[:, 3 * R:4 * R])
    sg_ref[0] = (g * jax.nn.sigmoid(g)).astype(BF16)
    a = _dot(h, wbf_ref[:, 4 * R:5 * R])
    bb = _dot(h, wbf_ref[:, 5 * R:6 * R])
    u_ref[0] = (a * jax.nn.sigmoid(bb)).astype(BF16)

    lgf = lgf_ref[...]
    row = lax.broadcasted_iota(jnp.int32, (CHUNK, R), 0).astype(F32)
    zeta = jnp.exp(lgf * (CHUNK - 1.0 - row))
    chunk_decay = jnp.exp(lgf * float(CHUNK))
    state = state_ref[...]
    for c in range(t // CHUNK):
        rf_ref[0, c] = state.astype(BF16)
        rows = slice(c * CHUNK, (c + 1) * CHUNK)
        kz = (k[rows] * zeta).astype(BF16)
        vc = v[rows]
        upd = jnp.concatenate(
            [_dot_tn(kz[:, hd * HEAD_DIM:(hd + 1) * HEAD_DIM],
                     vc[:, hd * HEAD_DIM:(hd + 1) * HEAD_DIM]) for hd in range(RET_HEADS)], axis=1)
        state = chunk_decay * state + upd
    state_ref[...] = state


def _in_proj(x, pos3, mix_norm_w, invf2, lgf_lanes, w_in, later_weights):
    b, s, d = x.shape
    t = IN_PROJ_TILE
    R = RET_WIDTH
    n_j = s // t
    n_steps = b * n_j
    tok = lambda i, j: (i, j, 0)
    act_spec = pl.BlockSpec((1, t, R), tok)
    act_shape = jax.ShapeDtypeStruct((b, s, R), BF16)
    cast_specs = []
    for w in later_weights:
        rows, cols = w.shape
        assert rows % (n_steps * BF16_SUBLANES) == 0
        cast_specs.append(pl.BlockSpec((rows // n_steps, cols), lambda i, j: (i * n_j + j, 0)))
    return pl.pallas_call(
        _in_proj_body,
        grid=(b, n_j),
        in_specs=[pl.BlockSpec((1, t, d), tok),
                  pl.BlockSpec((1, 1, t), lambda i, j: (i, 0, j)),
                  _const_spec((1, d)),
                  _const_spec((HEAD_DIM, 1)),
                  _const_spec((1, R)),
                  _const_spec((d, 6 * R))] + cast_specs,
        out_specs=([act_spec] * 5 + [pl.BlockSpec((1, t // CHUNK, CHUNK, R), lambda i, j: (i, j, 0, 0))]
                   + cast_specs),
        out_shape=([act_shape] * 5 + [jax.ShapeDtypeStruct((b, s // CHUNK, CHUNK, R), BF16)]
                   + [jax.ShapeDtypeStruct(w.shape, BF16) for w in later_weights]),
        scratch_shapes=[pltpu.VMEM((CHUNK, R), F32),
                        pltpu.VMEM((d, 6 * R), BF16)],
        compiler_params=pltpu.CompilerParams(
            dimension_semantics=("arbitrary", "arbitrary"), vmem_limit_bytes=VMEM_LIMIT_BYTES),
        name="in_proj",
    )(x, pos3, mix_norm_w, invf2, lgf_lanes, w_in, *later_weights)


def _conv_pieces(u_ref, up_ref, un_ref, has_prev, has_next,
                 cpar_ref, uext_ref, ush_ref, yconv_ref):
    t = u_ref.shape[1]
    span = CONV_SPAN + SHIFT_ROWS
    groups = (CONV_ROWS // SUBLANES, SUBLANES, CONV_WIDTH)
    block_vregs = CONV_ROWS * CONV_WIDTH // (SUBLANES * LANES)
    span_vregs = span * CONV_WIDTH // (SUBLANES * LANES)
    accs = {}

    def fill(gate):
        del gate
        zero_halo = jnp.zeros((HALO, CONV_WIDTH), F32)
        uext_ref[0:HALO, :] = jnp.where(has_prev, up_ref[0].astype(F32), zero_halo)
        uext_ref[HALO:HALO + t, :] = u_ref[0].astype(F32)
        uext_ref[HALO + t:HALO + t + HALO, :] = jnp.where(has_next, un_ref[0].astype(F32), zero_halo)

    def shift(hb, sft, gate):
        ush_ref[(hb // CONV_SPAN) % 2, sft, 0:span, :] = _gated(uext_ref[hb + sft:hb + sft + span, :], gate)

    def tap_step(hb, base, tap, gate):
        grp, sft = divmod(HALO - CONV_PAD + tap, SUBLANES)
        lo = base + grp * SUBLANES
        if sft == 0:
            src = uext_ref[hb + lo:hb + lo + CONV_ROWS, :]
        else:
            src = ush_ref[(hb // CONV_SPAN) % 2, sft, lo:lo + CONV_ROWS, :]
        acc = accs.pop((hb, base)) if tap else jnp.broadcast_to(cpar_ref[CONV_KERNEL][None], groups)
        weight = cpar_ref[tap]
        weight = _gated(weight, gate)
        acc = acc + weight[None] * src.reshape(groups)
        if tap < CONV_KERNEL - 1:
            accs[(hb, base)] = acc
            return
        mu = jnp.mean(acc, axis=-1, keepdims=True)
        yc = acc - mu
        var = jnp.mean(yc * yc, axis=-1, keepdims=True)
        yl = yc * lax.rsqrt(var + EPS) * cpar_ref[CONV_KERNEL + 1][None] + cpar_ref[CONV_KERNEL + 2][None]
        out = (yl * jax.nn.sigmoid(yl)).reshape(CONV_ROWS, CONV_WIDTH)
        yconv_ref[hb + base:hb + base + CONV_ROWS, :] = out.astype(BF16)

    def copies(hb):
        return [(3 * span_vregs, functools.partial(shift, hb, sft)) for sft in range(1, SUBLANES)]

    def taps(hb):
        return [(2 * block_vregs + (12 * block_vregs if tap == CONV_KERNEL - 1 else 0),
                 functools.partial(tap_step, hb, base, tap))
                for base in range(0, CONV_SPAN, CONV_ROWS) for tap in range(CONV_KERNEL)]

    work = [(2 * (t + 2 * HALO) * CONV_WIDTH // (SUBLANES * LANES), fill)] + copies(0)
    for hb in range(0, t, CONV_SPAN):
        ahead = copies(hb + CONV_SPAN) if hb + CONV_SPAN < t else []
        steps = taps(hb)
        every = len(steps) // (len(ahead) + 1)
        for i, step in enumerate(steps):
            work.append(step)
            if ahead and (i + 1) % every == 0:
                work.append(ahead.pop(0))
        assert not ahead
    return work


def _gated(x, gate):
    if gate is None:
        return x
    rows, cols = x.shape
    return x + jnp.tile(gate, (rows // SUBLANES, cols // LANES))


def _gates_from(y, n):
    rows, cols = y.shape
    tiles = [(r, c + dc) for c in range(0, cols, 2 * LANES) for r in range(0, rows, SUBLANES)
             for dc in (0, LANES)]
    picks = [tiles[(i * len(tiles)) // n] for i in range(n)]
    return [pltpu.bitcast((pltpu.bitcast(y[r:r + SUBLANES, c:c + LANES], jnp.uint32) >> 16) >> 16, F32)
            for r, c in picks]


def _rest_body(n_tiles,
               x_ref, q_ref, k_ref, v_ref, sg_ref, u_ref, up_ref, un_ref, u1_ref, up1_ref, un1_ref, rf_ref,
               lgf_ref, lgb_ref, gnw_ref, gnb_ref, cw_ref, cb_ref, lnw_ref, lnb_ref,
               wout_ref, xnw_ref, kt_ref, vm_ref, wxq_ref, wxo_ref,
               mnw_ref, w1_ref, w2_ref, fnw_ref,
               o_ref, rb_ref, mix_ref, uext_ref, ush_ref, yconv_ref, cpar_ref):
    t = x_ref.shape[1]
    R = RET_WIDTH
    step = pl.program_id(0)
    last_step = pl.num_programs(0) - 1
    tile = n_tiles - 1 - step % n_tiles
    step1 = jnp.minimum(step + 1, last_step)
    tile1 = n_tiles - 1 - step1 % n_tiles
    conv_consts = (cpar_ref, uext_ref, ush_ref, yconv_ref)

    @pl.when(step % n_tiles == 0)
    def _():
        rb_ref[...] = jnp.zeros_like(rb_ref)

    @pl.when(step == 0)
    def _():
        for tap in range(CONV_KERNEL):
            cpar_ref[tap] = jnp.broadcast_to(cw_ref[tap:tap + 1, :], (SUBLANES, CONV_WIDTH))
        for i, par in enumerate((cb_ref, lnw_ref, lnb_ref)):
            cpar_ref[CONV_KERNEL + i] = jnp.broadcast_to(par[...], (SUBLANES, CONV_WIDTH))
        for _, item in _conv_pieces(u_ref, up_ref, un_ref, tile > 0, tile < n_tiles - 1, *conv_consts):
            item(None)

    mix_ref[:, R:R + CONV_WIDTH] = yconv_ref[...]

    lgf = lgf_ref[...]
    lgb = lgb_ref[...]
    row = lax.broadcasted_iota(jnp.int32, (CHUNK, R), 0).astype(F32)
    xi_f = jnp.exp(lgf * (row + 1.0))
    xi_b = jnp.exp(lgb * (float(CHUNK) - row))
    zeta_b = jnp.exp(lgb * row)
    decay_b = jnp.exp(lgb * float(CHUNK))
    ii = lax.broadcasted_iota(jnp.int32, (CHUNK, CHUNK), 0)
    jj = lax.broadcasted_iota(jnp.int32, (CHUNK, CHUNK), 1)
    diff = (ii - jj).astype(F32)
    dmask = jnp.concatenate(
        [jnp.where(diff >= 0.0,
                   jnp.exp(lgf[:, hd * HEAD_DIM:(hd + 1) * HEAD_DIM] * jnp.maximum(diff, 0.0)),
                   jnp.exp(lgb[:, hd * HEAD_DIM:(hd + 1) * HEAD_DIM] * jnp.maximum(-diff, 0.0)))
         for hd in range(RET_HEADS)], axis=1)

    gnw = gnw_ref[...]
    gnb = gnb_ref[...]
    rb = rb_ref[...]
    for c in reversed(range(t // CHUNK)):
        rows = slice(c * CHUNK, (c + 1) * CHUNK)
        qc = q_ref[0, rows, :]
        kc = k_ref[0, rows, :]
        vc = v_ref[0, rows, :]
        sgc = sg_ref[0, rows, :].astype(F32)
        rfc = rf_ref[0, c]
        rbb = rb.astype(BF16)
        qf = qc.astype(F32)
        qxf = (qf * xi_f).astype(BF16)
        qxb = (qf * xi_b).astype(BF16)
        kzb = (kc.astype(F32) * zeta_b).astype(BF16)
        ys = []
        upds = []
        for ps in (slice(0, 2 * HEAD_DIM), slice(2 * HEAD_DIM, 4 * HEAD_DIM)):
            p = (_dot_nt(qc[:, ps], _block_diag(kc[:, ps])) * dmask[:, ps]).astype(BF16)
            lhs = jnp.concatenate([p, qxf[:, ps], qxb[:, ps]], axis=1)
            rhs = jnp.concatenate([_block_diag(vc[:, ps]), _block_diag(rfc[:, ps]), _block_diag(rbb[:, ps])],
                                  axis=0)
            ys.append(_dot(lhs, rhs))
            kv = _dot_tn(kzb[:, ps], vc[:, ps])
            upds += [kv[:HEAD_DIM, :HEAD_DIM], kv[HEAD_DIM:, HEAD_DIM:]]
        yn = []
        for hd, y in enumerate(ys):
            for half in (y[:, :HEAD_DIM], y[:, HEAD_DIM:]):
                mu = jnp.mean(half, axis=-1, keepdims=True)
                yc = half - mu
                var = jnp.mean(yc * yc, axis=-1, keepdims=True)
                yn.append(yc * lax.rsqrt(var + EPS))
        yn = jnp.concatenate(yn, axis=1) * gnw + gnb
        mix_ref[rows, 0:R] = (sgc * yn).astype(BF16)
        rb = decay_b * rb + jnp.concatenate(upds, axis=1)
    rb_ref[...] = rb

    work = _conv_pieces(u1_ref, up1_ref, un1_ref, tile1 > 0, tile1 < n_tiles - 1, *conv_consts)
    total_cost = sum(cost for cost, _ in work)
    n_chunks = D_FF // FF_CHUNK
    n_shares = 3 + 2 * n_chunks - 1
    progress = {"share": 0, "item": 0, "cost": 0}

    def conv_share(result):
        progress["share"] += 1
        items = []
        while progress["item"] < len(work) and progress["cost"] * n_shares < total_cost * progress["share"]:
            cost, item = work[progress["item"]]
            items.append(item)
            progress["item"] += 1
            progress["cost"] += cost
        for item, gate in zip(items, _gates_from(result, len(items))):
            item(gate)
        return result

    x1 = x_ref[0] + conv_share(_dot(mix_ref[...], wout_ref[...]))

    qx = _dot((x1 * xnw_ref[...]).astype(BF16), wxq_ref[...])
    qx = (conv_share(qx) * _inv_rms(x1)).astype(BF16)
    outs = []
    for hd in range(XATTN_HEADS):
        hs = slice(hd * XATTN_HEAD_DIM, (hd + 1) * XATTN_HEAD_DIM)
        s = _dot(qx[:, hs], kt_ref[0, hs, :])
        e = jnp.exp(s - jnp.max(s, axis=-1, keepdims=True))
        p = e * (1.0 / jnp.sum(e, axis=-1, keepdims=True))
        outs.append(_dot(p.astype(BF16), vm_ref[0, :, hs]).astype(BF16))
    x2 = x1 + conv_share(_dot(jnp.concatenate(outs, axis=1), wxo_ref[...]))

    hm = (x2 * mnw_ref[...]).astype(BF16)
    acc = jnp.zeros((t, D_MODEL), F32)
    for c in range(n_chunks):
        cs = slice(c * FF_CHUNK, (c + 1) * FF_CHUNK)
        hc = conv_share(jnp.maximum(_dot(hm, w1_ref[:, cs]), 0.0))
        part = _dot((hc * hc).astype(BF16), w2_ref[cs, :])
        acc = acc + (conv_share(part) if c < n_chunks - 1 else part)
    assert progress["item"] == len(work) and progress["share"] == n_shares
    scale = _inv_rms(x2)
    o_ref[0] = _rms(x2 + acc * (scale * scale), fnw_ref[...])


def _block_rest(x, q, k, v, sg, u, rf, lgf_lanes, lgb_lanes, gn_w, gn_b, conv_w, conv_b, ln_w, ln_b,
                w_out, xattn_norm_w, kt, vm, w_xq, w_xo, mlp_norm_w, w_ff1, w_ff2, final_norm_w):
    b, s, d = x.shape
    t = REST_TILE
    R = RET_WIDTH
    n_tiles = s // t
    n_steps = b * n_tiles
    hb = t // HALO
    n_hb = s // HALO

    def batch_tile(step):
        return step // n_tiles, n_tiles - 1 - step % n_tiles

    def ahead(step):
        return jnp.minimum(step + 1, n_steps - 1)

    def main_map(step):
        bi, ti = batch_tile(step)
        return bi, ti, 0

    def prev_halo_map(step):
        bi, ti = batch_tile(step)
        return bi, jnp.maximum(ti * hb - 1, 0), 0

    def next_halo_map(step):
        bi, ti = batch_tile(step)
        return bi, jnp.minimum((ti + 1) * hb, n_hb - 1), 0

    act_spec = pl.BlockSpec((1, t, R), main_map)
    halo_shape = (1, HALO, R)
    m = kt.shape[2]
    return pl.pallas_call(
        functools.partial(_rest_body, n_tiles),
        grid=(n_steps,),
        in_specs=[pl.BlockSpec((1, t, d), main_map),
                  act_spec, act_spec, act_spec, act_spec, act_spec,
                  pl.BlockSpec(halo_shape, prev_halo_map),
                  pl.BlockSpec(halo_shape, next_halo_map),
                  pl.BlockSpec((1, t, R), lambda st: main_map(ahead(st))),
                  pl.BlockSpec(halo_shape, lambda st: prev_halo_map(ahead(st))),
                  pl.BlockSpec(halo_shape, lambda st: next_halo_map(ahead(st))),
                  pl.BlockSpec((1, t // CHUNK, CHUNK, R), lambda st: main_map(st) + (0,)),
                  _const_spec((1, R)), _const_spec((1, R)),
                  _const_spec((1, R)), _const_spec((1, R)),
                  _const_spec((CONV_KERNEL, CONV_WIDTH)),
                  _const_spec((1, CONV_WIDTH)), _const_spec((1, CONV_WIDTH)), _const_spec((1, CONV_WIDTH)),
                  _const_spec((d, d)),
                  _const_spec((1, d)),
                  pl.BlockSpec((1, d, m), lambda st: (st // n_tiles, 0, 0)),
                  pl.BlockSpec((1, m, d), lambda st: (st // n_tiles, 0, 0)),
                  _const_spec((d, d)), _const_spec((d, d)),
                  _const_spec((1, d)),
                  _const_spec((d, D_FF)), _const_spec((D_FF, d)),
                  _const_spec((1, d))],
        out_specs=pl.BlockSpec((1, t, d), main_map),
        out_shape=jax.ShapeDtypeStruct((b, s, d), F32),
        scratch_shapes=[pltpu.VMEM((CHUNK, R), F32),
                        pltpu.VMEM((t, D_MODEL), BF16),
                        pltpu.VMEM((t + 2 * HALO, CONV_WIDTH), F32),
                        pltpu.VMEM((2, SUBLANES, CONV_SPAN + SHIFT_ROWS, CONV_WIDTH), F32),
                        pltpu.VMEM((t, CONV_WIDTH), BF16),
                        pltpu.VMEM((CONV_KERNEL + 3, SUBLANES, CONV_WIDTH), F32)],
        compiler_params=pltpu.CompilerParams(
            dimension_semantics=("arbitrary",), vmem_limit_bytes=VMEM_LIMIT_BYTES),
        name="block_rest",
    )(x, q, k, v, sg, u, u, u, u, u, u, rf, lgf_lanes, lgb_lanes, gn_w, gn_b, conv_w, conv_b, ln_w, ln_b,
      w_out, xattn_norm_w, kt, vm, w_xq, w_xo, mlp_norm_w, w_ff1, w_ff2, final_norm_w)


def kernel(x, mem, positions, mix_norm_w, w_in, ret_decay_f, ret_decay_b, ret_gn_w, ret_gn_b,
           conv_w, conv_b, conv_ln_w, conv_ln_b, w_out, xattn_norm_w, mem_norm_w,
           w_xq, w_xkv, w_xo, mlp_norm_w, w_ff1, w_ff2, final_norm_w):
    assert w_in.shape[0] == 1, "single-layer block"
    b, s, d = x.shape
    half = HEAD_DIM // 2
    inv_freq = ROPE_THETA ** (-jnp.arange(half, dtype=F32) / half)
    invf2 = jnp.concatenate([inv_freq, inv_freq])[:, None]
    pos3 = positions.reshape(b, 1, s)
    row = lambda a: a.reshape(1, -1)
    lgf_lanes = jnp.repeat(jax.nn.log_sigmoid(ret_decay_f[0].astype(F32)), HEAD_DIM)[None]
    lgb_lanes = jnp.repeat(jax.nn.log_sigmoid(ret_decay_b[0].astype(F32)), HEAD_DIM)[None]
    kt, vm = _mem_kv(mem, row(mem_norm_w[0]), w_xkv[0])
    (q, k, v, sg, u, rf, w_out_b, w_xq_b, w_xo_b, w_ff1_b, w_ff2_b) = _in_proj(
        x, pos3, row(mix_norm_w[0]), invf2, lgf_lanes, w_in[0],
        (w_out[0], w_xq[0], w_xo[0], w_ff1[0], w_ff2[0]))
    return _block_rest(x, q, k, v, sg, u, rf, lgf_lanes, lgb_lanes,
                       row(ret_gn_w[0]), row(ret_gn_b[0]),
                       conv_w[0].reshape(CONV_KERNEL, CONV_WIDTH), row(conv_b[0]),
                       row(conv_ln_w[0]), row(conv_ln_b[0]),
                       w_out_b, row(xattn_norm_w[0]), kt, vm, w_xq_b, w_xo_b,
                       row(mlp_norm_w[0]), w_ff1_b, w_ff2_b, row(final_norm_w))
```

```python
import functools

import jax
import jax.numpy as jnp
import numpy as np
from jax import lax
from jax.experimental import pallas as pl
from jax.experimental.pallas import tpu as pltpu

F32 = jnp.float32
BF16 = jnp.bfloat16

D_MODEL = 1024
RET_WIDTH = 512
CONV_WIDTH = 512
RET_HEADS = 4
HEAD_DIM = 128
CHUNK = 128
CONV_KERNEL = 31
CONV_PAD = CONV_KERNEL // 2
HALO = 16
XATTN_HEADS = 4
XATTN_HEAD_DIM = 256
D_FF = 4096
ROPE_THETA = 10000.0
EPS = 1e-6

IN_PROJ_TILE = 1024
X_SLOTS = 3
REST_TILE = 512
FF_CHUNK = 512
CONV_ROWS = 32
CONV_SPAN = 128
CAST_ROWS = 128
HALF_PI_1 = 1.5703125
HALF_PI_2 = 4.837512969970703125e-4
HALF_PI_3 = 7.54978995489188216e-8
SUBLANES = 8
BF16_SUBLANES = 16
LANES = 128
SHIFT_ROWS = ((HALO - CONV_PAD + CONV_KERNEL - 1) // SUBLANES) * SUBLANES
VMEM_LIMIT_BYTES = 58 * 1024 * 1024


def _inv_rms(x):
    return lax.rsqrt(jnp.mean(x * x, axis=-1, keepdims=True) + EPS)


def _rms(x, w):
    return x * _inv_rms(x) * w


def _dot(a, b):
    return jnp.dot(a, b, preferred_element_type=F32)


def _dot_tn(a, b):
    return lax.dot_general(a, b, (((0,), (0,)), ((), ())), preferred_element_type=F32)


def _dot_nt(a, b):
    return lax.dot_general(a, b, (((1,), (1,)), ((), ())), preferred_element_type=F32)


def _block_diag(a):
    n, w2 = a.shape
    zero = jnp.zeros((n, w2 // 2), a.dtype)
    return jnp.concatenate([jnp.concatenate([a[:, :w2 // 2], zero], axis=1),
                            jnp.concatenate([zero, a[:, w2 // 2:]], axis=1)], axis=0)


def _const_spec(shape):
    nd = len(shape)
    return pl.BlockSpec(shape, lambda *_: (0,) * nd, pipeline_mode=pl.Buffered(1))


def _mem_kv_body(mem_ref, nw_ref, wkv_ref, kt_ref, v_ref):
    m = _rms(mem_ref[0], nw_ref[...]).astype(BF16)
    kv = _dot(m, wkv_ref[...].astype(BF16))
    k = kv[:, :D_MODEL] * (XATTN_HEAD_DIM ** -0.5)
    kt_ref[0] = k.T.astype(BF16)
    v_ref[0] = kv[:, D_MODEL:].astype(BF16)


def _mem_kv(mem, mem_norm_w, w_xkv):
    b, m, d = mem.shape
    return pl.pallas_call(
        _mem_kv_body,
        grid=(b,),
        in_specs=[pl.BlockSpec((1, m, d), lambda i: (i, 0, 0)),
                  _const_spec((1, d)),
                  _const_spec((d, 2 * d))],
        out_specs=[pl.BlockSpec((1, d, m), lambda i: (i, 0, 0)),
                   pl.BlockSpec((1, m, d), lambda i: (i, 0, 0))],
        out_shape=[jax.ShapeDtypeStruct((b, d, m), BF16),
                   jax.ShapeDtypeStruct((b, m, d), BF16)],
        compiler_params=pltpu.CompilerParams(
            dimension_semantics=("arbitrary",), vmem_limit_bytes=VMEM_LIMIT_BYTES),
        name="mem_kv",
    )(mem, mem_norm_w, w_xkv)


def _cos_or_sin(x, want_sin):
    k = jnp.floor(x * (2.0 / np.pi) + 0.5)
    r = ((x - k * HALF_PI_1) - k * HALF_PI_2) - k * HALF_PI_3
    r2 = r * r
    sin_r = r + r * r2 * (-1.6666654611e-1 + r2 * (8.3321608736e-3 + r2 * -1.9515295891e-4))
    cos_r = 1.0 - 0.5 * r2 + r2 * r2 * (4.166664568298827e-2
                                        + r2 * (-1.388731625493765e-3 + r2 * 2.443315711809948e-5))
    kq = k + jnp.where(want_sin, 3.0, 0.0)
    q = kq - 4.0 * jnp.floor(kq * 0.25)
    hi = jnp.floor(q * 0.5)
    lo = q - 2.0 * hi
    return (1.0 - 2.0 * hi) * (cos_r - lo * (cos_r + sin_r))


def _in_proj_body(x_ref, pos_ref, nw_ref, invf_ref, lgf_ref, w_ref,
                  wout_ref, wxq_ref, wxo_ref, w1_ref, w2_ref,
                  q_ref, k_ref, v_ref, sg_ref, u_ref, rf_ref,
                  wout_o, wxq_o, wxo_o, w1_o, w2_o, state_ref, wbf_ref, xbuf_ref, xsem):
    t = xbuf_ref.shape[1]
    R = RET_WIDTH

    n_j = pl.num_programs(1)
    n_steps = pl.num_programs(0) * n_j
    step = pl.program_id(0) * n_j + pl.program_id(1)

    def x_copy(s):
        return pltpu.make_async_copy(x_ref.at[s // n_j, pl.ds((s % n_j) * t, t), :],
                                     xbuf_ref.at[s % X_SLOTS], xsem.at[s % X_SLOTS])

    @pl.when(step == 0)
    def _():
        for s in range(X_SLOTS - 1):
            x_copy(s).start()

    @pl.when(step + (X_SLOTS - 1) < n_steps)
    def _():
        x_copy(step + (X_SLOTS - 1)).start()

    x_copy(step).wait()

    @pl.when((pl.program_id(0) == 0) & (pl.program_id(1) == 0))
    def _():
        for r0 in range(0, D_MODEL, CAST_ROWS):
            wbf_ref[r0:r0 + CAST_ROWS, :] = w_ref[r0:r0 + CAST_ROWS, :].astype(BF16)

    @pl.when(pl.program_id(1) == 0)
    def _():
        state_ref[...] = jnp.zeros_like(state_ref)

    for src, dst in ((wout_ref, wout_o), (wxq_ref, wxq_o), (wxo_ref, wxo_o), (w1_ref, w1_o), (w2_ref, w2_o)):
        dst[...] = src[...].astype(BF16)

    h = _rms(xbuf_ref[step % X_SLOTS], nw_ref[...]).astype(BF16)

    ang_t = invf_ref[...] * pos_ref[0].astype(F32)
    cs = _cos_or_sin(ang_t, lax.broadcasted_iota(jnp.int32, (HEAD_DIM, 1), 0) >= HEAD_DIM // 2).T
    upper = lax.broadcasted_iota(jnp.int32, (1, HEAD_DIM), 1) >= HEAD_DIM // 2
    sc = pltpu.roll(cs, HEAD_DIM // 2, 1)
    cos2 = jnp.where(upper, sc, cs)
    sin2 = jnp.where(upper, cs, -sc)

    def rope(p):
        outs = []
        for hd in range(RET_HEADS):
            ph = p[:, hd * HEAD_DIM:(hd + 1) * HEAD_DIM]
            outs.append(ph * cos2 + pltpu.roll(ph, HEAD_DIM // 2, 1) * sin2)
        return jnp.concatenate(outs, axis=1)

    q = rope(_dot(h, wbf_ref[:, 0:R]))
    q_ref[0] = q.astype(BF16)
    k = rope(_dot(h, wbf_ref[:, R:2 * R])) * (HEAD_DIM ** -0.5)
    k_ref[0] = k.astype(BF16)
    v = _dot(h, wbf_ref[:, 2 * R:3 * R]).astype(BF16)
    v_ref[0] = v
    g = _dot(h, wbf_ref[:, 3 * R:4 * R])
    sg_ref[0] = (g * jax.nn.sigmoid(g)).astype(BF16)
    a = _dot(h, wbf_ref[:, 4 * R:5 * R])
    bb = _dot(h, wbf_ref[:, 5 * R:6 * R])
    u_ref[0] = (a * jax.nn.sigmoid(bb)).astype(BF16)

    lgf = lgf_ref[...]
    row = lax.broadcasted_iota(jnp.int32, (CHUNK, R), 0).astype(F32)
    zeta = jnp.exp(lgf * (CHUNK - 1.0 - row))
    chunk_decay = jnp.exp(lgf * float(CHUNK))
    state = state_ref[...]
    for c in range(t // CHUNK):
        rf_ref[0, c] = state.astype(BF16)
        rows = slice(c * CHUNK, (c + 1) * CHUNK)
        kz = (k[rows] * zeta).astype(BF16)
        vc = v[rows]
        upd = jnp.concatenate(
            [_dot_tn(kz[:, hd * HEAD_DIM:(hd + 1) * HEAD_DIM],
                     vc[:, hd * HEAD_DIM:(hd + 1) * HEAD_DIM]) for hd in range(RET_HEADS)], axis=1)
        state = chunk_decay * state + upd
    state_ref[...] = state


def _in_proj(x, pos3, mix_norm_w, invf2, lgf_lanes, w_in, later_weights):
    b, s, d = x.shape
    t = IN_PROJ_TILE
    R = RET_WIDTH
    n_j = s // t
    n_steps = b * n_j
    tok = lambda i, j: (i, j, 0)
    act_spec = pl.BlockSpec((1, t, R), tok)
    act_shape = jax.ShapeDtypeStruct((b, s, R), BF16)
    cast_specs = []
    for w in later_weights:
        rows, cols = w.shape
        assert rows % (n_steps * BF16_SUBLANES) == 0
        cast_specs.append(pl.BlockSpec((rows // n_steps, cols), lambda i, j: (i * n_j + j, 0)))
    return pl.pallas_call(
        _in_proj_body,
        grid=(b, n_j),
        in_specs=[pl.BlockSpec(memory_space=pl.ANY),
                  pl.BlockSpec((1, 1, t), lambda i, j: (i, 0, j)),
                  _const_spec((1, d)),
                  _const_spec((HEAD_DIM, 1)),
                  _const_spec((1, R)),
                  _const_spec((d, 6 * R))] + cast_specs,
        out_specs=([act_spec] * 5 + [pl.BlockSpec((1, t // CHUNK, CHUNK, R), lambda i, j: (i, j, 0, 0))]
                   + cast_specs),
        out_shape=([act_shape] * 5 + [jax.ShapeDtypeStruct((b, s // CHUNK, CHUNK, R), BF16)]
                   + [jax.ShapeDtypeStruct(w.shape, BF16) for w in later_weights]),
        scratch_shapes=[pltpu.VMEM((CHUNK, R), F32),
                        pltpu.VMEM((d, 6 * R), BF16),
                        pltpu.VMEM((X_SLOTS, t, d), F32),
                        pltpu.SemaphoreType.DMA((X_SLOTS,))],
        compiler_params=pltpu.CompilerParams(
            dimension_semantics=("arbitrary", "arbitrary"), vmem_limit_bytes=VMEM_LIMIT_BYTES),
        name="in_proj",
    )(x, pos3, mix_norm_w, invf2, lgf_lanes, w_in, *later_weights)


def _conv_pieces(u_ref, up_ref, un_ref, has_prev, has_next,
                 cpar_ref, uext_ref, ush_ref, yconv_ref):
    t = u_ref.shape[1]
    span = CONV_SPAN + SHIFT_ROWS
    groups = (CONV_ROWS // SUBLANES, SUBLANES, CONV_WIDTH)
    block_vregs = CONV_ROWS * CONV_WIDTH // (SUBLANES * LANES)
    span_vregs = span * CONV_WIDTH // (SUBLANES * LANES)
    accs = {}

    def fill(gate):
        del gate
        zero_halo = jnp.zeros((HALO, CONV_WIDTH), F32)
        uext_ref[0:HALO, :] = jnp.where(has_prev, up_ref[0].astype(F32), zero_halo)
        uext_ref[HALO:HALO + t, :] = u_ref[0].astype(F32)
        uext_ref[HALO + t:HALO + t + HALO, :] = jnp.where(has_next, un_ref[0].astype(F32), zero_halo)

    def shift(hb, sft, gate):
        ush_ref[(hb // CONV_SPAN) % 2, sft, 0:span, :] = _gated(uext_ref[hb + sft:hb + sft + span, :], gate)

    def tap_step(hb, base, tap, gate):
        grp, sft = divmod(HALO - CONV_PAD + tap, SUBLANES)
        lo = base + grp * SUBLANES
        if sft == 0:
            src = uext_ref[hb + lo:hb + lo + CONV_ROWS, :]
        else:
            src = ush_ref[(hb // CONV_SPAN) % 2, sft, lo:lo + CONV_ROWS, :]
        acc = accs.pop((hb, base)) if tap else jnp.broadcast_to(cpar_ref[CONV_KERNEL][None], groups)
        weight = cpar_ref[tap]
        weight = _gated(weight, gate)
        acc = acc + weight[None] * src.reshape(groups)
        if tap < CONV_KERNEL - 1:
            accs[(hb, base)] = acc
            return
        mu = jnp.mean(acc, axis=-1, keepdims=True)
        yc = acc - mu
        var = jnp.mean(yc * yc, axis=-1, keepdims=True)
        yl = yc * lax.rsqrt(var + EPS) * cpar_ref[CONV_KERNEL + 1][None] + cpar_ref[CONV_KERNEL + 2][None]
        out = (yl * jax.nn.sigmoid(yl)).reshape(CONV_ROWS, CONV_WIDTH)
        yconv_ref[hb + base:hb + base + CONV_ROWS, :] = out.astype(BF16)

    def copies(hb):
        return [(3 * span_vregs, functools.partial(shift, hb, sft)) for sft in range(1, SUBLANES)]

    def taps(hb):
        return [(2 * block_vregs + (12 * block_vregs if tap == CONV_KERNEL - 1 else 0),
                 functools.partial(tap_step, hb, base, tap))
                for base in range(0, CONV_SPAN, CONV_ROWS) for tap in range(CONV_KERNEL)]

    work = [(2 * (t + 2 * HALO) * CONV_WIDTH // (SUBLANES * LANES), fill)] + copies(0)
    for hb in range(0, t, CONV_SPAN):
        ahead = copies(hb + CONV_SPAN) if hb + CONV_SPAN < t else []
        steps = taps(hb)
        every = len(steps) // (len(ahead) + 1)
        for i, step in enumerate(steps):
            work.append(step)
            if ahead and (i + 1) % every == 0:
                work.append(ahead.pop(0))
        assert not ahead
    return work


def _gated(x, gate):
    if gate is None:
        return x
    rows, cols = x.shape
    return x + jnp.tile(gate, (rows // SUBLANES, cols // LANES))


def _gates_from(y, n):
    rows, cols = y.shape
    tiles = [(r, c + dc) for c in range(0, cols, 2 * LANES) for r in range(0, rows, SUBLANES)
             for dc in (0, LANES)]
    picks = [tiles[(i * len(tiles)) // n] for i in range(n)]
    return [pltpu.bitcast((pltpu.bitcast(y[r:r + SUBLANES, c:c + LANES], jnp.uint32) >> 16) >> 16, F32)
            for r, c in picks]


def _rest_body(n_tiles,
               x_ref, q_ref, k_ref, v_ref, sg_ref, u_ref, up_ref, un_ref, u1_ref, up1_ref, un1_ref, rf_ref,
               lgf_ref, lgb_ref, gnw_ref, gnb_ref, cw_ref, cb_ref, lnw_ref, lnb_ref,
               wout_ref, xnw_ref, kt_ref, vm_ref, wxq_ref, wxo_ref,
               mnw_ref, w1_ref, w2_ref, fnw_ref,
               o_ref, rb_ref, mix_ref, uext_ref, ush_ref, yconv_ref, cpar_ref):
    t = x_ref.shape[1]
    R = RET_WIDTH
    step = pl.program_id(0)
    last_step = pl.num_programs(0) - 1
    tile = n_tiles - 1 - step % n_tiles
    step1 = jnp.minimum(step + 1, last_step)
    tile1 = n_tiles - 1 - step1 % n_tiles
    conv_consts = (cpar_ref, uext_ref, ush_ref, yconv_ref)

    @pl.when(step % n_tiles == 0)
    def _():
        rb_ref[...] = jnp.zeros_like(rb_ref)

    @pl.when(step == 0)
    def _():
        for tap in range(CONV_KERNEL):
            cpar_ref[tap] = jnp.broadcast_to(cw_ref[tap:tap + 1, :], (SUBLANES, CONV_WIDTH))
        for i, par in enumerate((cb_ref, lnw_ref, lnb_ref)):
            cpar_ref[CONV_KERNEL + i] = jnp.broadcast_to(par[...], (SUBLANES, CONV_WIDTH))
        for _, item in _conv_pieces(u_ref, up_ref, un_ref, tile > 0, tile < n_tiles - 1, *conv_consts):
            item(None)

    mix_ref[:, R:R + CONV_WIDTH] = yconv_ref[...]

    lgf = lgf_ref[...]
    lgb = lgb_ref[...]
    row = lax.broadcasted_iota(jnp.int32, (CHUNK, R), 0).astype(F32)
    xi_f = jnp.exp(lgf * (row + 1.0))
    xi_b = jnp.exp(lgb * (float(CHUNK) - row))
    zeta_b = jnp.exp(lgb * row)
    decay_b = jnp.exp(lgb * float(CHUNK))
    ii = lax.broadcasted_iota(jnp.int32, (CHUNK, CHUNK), 0)
    jj = lax.broadcasted_iota(jnp.int32, (CHUNK, CHUNK), 1)
    diff = (ii - jj).astype(F32)
    dmask = jnp.concatenate(
        [jnp.where(diff >= 0.0,
                   jnp.exp(lgf[:, hd * HEAD_DIM:(hd + 1) * HEAD_DIM] * jnp.maximum(diff, 0.0)),
                   jnp.exp(lgb[:, hd * HEAD_DIM:(hd + 1) * HEAD_DIM] * jnp.maximum(-diff, 0.0)))
         for hd in range(RET_HEADS)], axis=1)

    gnw = gnw_ref[...]
    gnb = gnb_ref[...]
    rb = rb_ref[...]
    for c in reversed(range(t // CHUNK)):
        rows = slice(c * CHUNK, (c + 1) * CHUNK)
        qc = q_ref[0, rows, :]
        kc = k_ref[0, rows, :]
        vc = v_ref[0, rows, :]
        sgc = sg_ref[0, rows, :].astype(F32)
        rfc = rf_ref[0, c]
        rbb = rb.astype(BF16)
        qf = qc.astype(F32)
        qxf = (qf * xi_f).astype(BF16)
        qxb = (qf * xi_b).astype(BF16)
        kzb = (kc.astype(F32) * zeta_b).astype(BF16)
        ys = []
        upds = []
        for ps in (slice(0, 2 * HEAD_DIM), slice(2 * HEAD_DIM, 4 * HEAD_DIM)):
            p = (_dot_nt(qc[:, ps], _block_diag(kc[:, ps])) * dmask[:, ps]).astype(BF16)
            lhs = jnp.concatenate([p, qxf[:, ps], qxb[:, ps]], axis=1)
            rhs = jnp.concatenate([_block_diag(vc[:, ps]), _block_diag(rfc[:, ps]), _block_diag(rbb[:, ps])],
                                  axis=0)
            ys.append(_dot(lhs, rhs))
            kv = _dot_tn(kzb[:, ps], vc[:, ps])
            upds += [kv[:HEAD_DIM, :HEAD_DIM], kv[HEAD_DIM:, HEAD_DIM:]]
        yn = []
        for hd, y in enumerate(ys):
            for half in (y[:, :HEAD_DIM], y[:, HEAD_DIM:]):
                mu = jnp.mean(half, axis=-1, keepdims=True)
                yc = half - mu
                var = jnp.mean(yc * yc, axis=-1, keepdims=True)
                yn.append(yc * lax.rsqrt(var + EPS))
        yn = jnp.concatenate(yn, axis=1) * gnw + gnb
        mix_ref[rows, 0:R] = (sgc * yn).astype(BF16)
        rb = decay_b * rb + jnp.concatenate(upds, axis=1)
    rb_ref[...] = rb

    work = _conv_pieces(u1_ref, up1_ref, un1_ref, tile1 > 0, tile1 < n_tiles - 1, *conv_consts)
    total_cost = sum(cost for cost, _ in work)
    n_chunks = D_FF // FF_CHUNK
    n_shares = 3 + 2 * n_chunks - 1
    progress = {"share": 0, "item": 0, "cost": 0}

    def conv_share(result):
        progress["share"] += 1
        items = []
        while progress["item"] < len(work) and progress["cost"] * n_shares < total_cost * progress["share"]:
            cost, item = work[progress["item"]]
            items.append(item)
            progress["item"] += 1
            progress["cost"] += cost
        for item, gate in zip(items, _gates_from(result, len(items))):
            item(gate)
        return result

    x1 = x_ref[0] + conv_share(_dot(mix_ref[...], wout_ref[...]))

    qx = _dot((x1 * xnw_ref[...]).astype(BF16), wxq_ref[...])
    qx = (conv_share(qx) * _inv_rms(x1)).astype(BF16)
    outs = []
    for hd in range(XATTN_HEADS):
        hs = slice(hd * XATTN_HEAD_DIM, (hd + 1) * XATTN_HEAD_DIM)
        s = _dot(qx[:, hs], kt_ref[0, hs, :])
        e = jnp.exp(s - jnp.max(s, axis=-1, keepdims=True))
        p = e * (1.0 / jnp.sum(e, axis=-1, keepdims=True))
        outs.append(_dot(p.astype(BF16), vm_ref[0, :, hs]).astype(BF16))
    x2 = x1 + conv_share(_dot(jnp.concatenate(outs, axis=1), wxo_ref[...]))

    hm = (x2 * mnw_ref[...]).astype(BF16)
    acc = jnp.zeros((t, D_MODEL), F32)
    for c in range(n_chunks):
        cs = slice(c * FF_CHUNK, (c + 1) * FF_CHUNK)
        hc = conv_share(jnp.maximum(_dot(hm, w1_ref[:, cs]), 0.0))
        part = _dot((hc * hc).astype(BF16), w2_ref[cs, :])
        acc = acc + (conv_share(part) if c < n_chunks - 1 else part)
    assert progress["item"] == len(work) and progress["share"] == n_shares
    scale = _inv_rms(x2)
    o_ref[0] = _rms(x2 + acc * (scale * scale), fnw_ref[...])


def _block_rest(x, q, k, v, sg, u, rf, lgf_lanes, lgb_lanes, gn_w, gn_b, conv_w, conv_b, ln_w, ln_b,
                w_out, xattn_norm_w, kt, vm, w_xq, w_xo, mlp_norm_w, w_ff1, w_ff2, final_norm_w):
    b, s, d = x.shape
    t = REST_TILE
    R = RET_WIDTH
    n_tiles = s // t
    n_steps = b * n_tiles
    hb = t // HALO
    n_hb = s // HALO

    def batch_tile(step):
        return step // n_tiles, n_tiles - 1 - step % n_tiles

    def ahead(step):
        return jnp.minimum(step + 1, n_steps - 1)

    def main_map(step):
        bi, ti = batch_tile(step)
        return bi, ti, 0

    def prev_halo_map(step):
        bi, ti = batch_tile(step)
        return bi, jnp.maximum(ti * hb - 1, 0), 0

    def next_halo_map(step):
        bi, ti = batch_tile(step)
        return bi, jnp.minimum((ti + 1) * hb, n_hb - 1), 0

    act_spec = pl.BlockSpec((1, t, R), main_map)
    halo_shape = (1, HALO, R)
    m = kt.shape[2]
    return pl.pallas_call(
        functools.partial(_rest_body, n_tiles),
        grid=(n_steps,),
        in_specs=[pl.BlockSpec((1, t, d), main_map),
                  act_spec, act_spec, act_spec, act_spec, act_spec,
                  pl.BlockSpec(halo_shape, prev_halo_map),
                  pl.BlockSpec(halo_shape, next_halo_map),
                  pl.BlockSpec((1, t, R), lambda st: main_map(ahead(st))),
                  pl.BlockSpec(halo_shape, lambda st: prev_halo_map(ahead(st))),
                  pl.BlockSpec(halo_shape, lambda st: next_halo_map(ahead(st))),
                  pl.BlockSpec((1, t // CHUNK, CHUNK, R), lambda st: main_map(st) + (0,)),
                  _const_spec((1, R)), _const_spec((1, R)),
                  _const_spec((1, R)), _const_spec((1, R)),
                  _const_spec((CONV_KERNEL, CONV_WIDTH)),
                  _const_spec((1, CONV_WIDTH)), _const_spec((1, CONV_WIDTH)), _const_spec((1, CONV_WIDTH)),
                  _const_spec((d, d)),
                  _const_spec((1, d)),
                  pl.BlockSpec((1, d, m), lambda st: (st // n_tiles, 0, 0)),
                  pl.BlockSpec((1, m, d), lambda st: (st // n_tiles, 0, 0)),
                  _const_spec((d, d)), _const_spec((d, d)),
                  _const_spec((1, d)),
                  _const_spec((d, D_FF)), _const_spec((D_FF, d)),
                  _const_spec((1, d))],
        out_specs=pl.BlockSpec((1, t, d), main_map),
        out_shape=jax.ShapeDtypeStruct((b, s, d), F32),
        scratch_shapes=[pltpu.VMEM((CHUNK, R), F32),
                        pltpu.VMEM((t, D_MODEL), BF16),
                        pltpu.VMEM((t + 2 * HALO, CONV_WIDTH), F32),
                        pltpu.VMEM((2, SUBLANES, CONV_SPAN + SHIFT_ROWS, CONV_WIDTH), F32),
                        pltpu.VMEM((t, CONV_WIDTH), BF16),
                        pltpu.VMEM((CONV_KERNEL + 3, SUBLANES, CONV_WIDTH), F32)],
        compiler_params=pltpu.CompilerParams(
            dimension_semantics=("arbitrary",), vmem_limit_bytes=VMEM_LIMIT_BYTES),
        name="block_rest",
    )(x, q, k, v, sg, u, u, u, u, u, u, rf, lgf_lanes, lgb_lanes, gn_w, gn_b, conv_w, conv_b, ln_w, ln_b,
      w_out, xattn_norm_w, kt, vm, w_xq, w_xo, mlp_norm_w, w_ff1, w_ff2, final_norm_w)


def kernel(x, mem, positions, mix_norm_w, w_in, ret_decay_f, ret_decay_b, ret_gn_w, ret_gn_b,
           conv_w, conv_b, conv_ln_w, conv_ln_b, w_out, xattn_norm_w, mem_norm_w,
           w_xq, w_xkv, w_xo, mlp_norm_w, w_ff1, w_ff2, final_norm_w):
    assert w_in.shape[0] == 1, "single-layer block"
    b, s, d = x.shape
    half = HEAD_DIM // 2
    inv_freq = ROPE_THETA ** (-jnp.arange(half, dtype=F32) / half)
    invf2 = jnp.concatenate([inv_freq, inv_freq])[:, None]
    pos3 = positions.reshape(b, 1, s)
    row = lambda a: a.reshape(1, -1)
    lgf_lanes = jnp.repeat(jax.nn.log_sigmoid(ret_decay_f[0].astype(F32)), HEAD_DIM)[None]
    lgb_lanes = jnp.repeat(jax.nn.log_sigmoid(ret_decay_b[0].astype(F32)), HEAD_DIM)[None]
    kt, vm = _mem_kv(mem, row(mem_norm_w[0]), w_xkv[0])
    (q, k, v, sg, u, rf, w_out_b, w_xq_b, w_xo_b, w_ff1_b, w_ff2_b) = _in_proj(
        x, pos3, row(mix_norm_w[0]), invf2, lgf_lanes, w_in[0],
        (w_out[0], w_xq[0], w_xo[0], w_ff1[0], w_ff2[0]))
    return _block_rest(x, q, k, v, sg, u, rf, lgf_lanes, lgb_lanes,
                       row(ret_gn_w[0]), row(ret_gn_b[0]),
                       conv_w[0].reshape(CONV_KERNEL, CONV_WIDTH), row(conv_b[0]),
                       row(conv_ln_w[0]), row(conv_ln_b[0]),
                       w_out_b, row(xattn_norm_w[0]), kt, vm, w_xq_b, w_xo_b,
                       row(mlp_norm_w[0]), w_ff1_b, w_ff2_b, row(final_norm_w))
```
